```python
import math
import jax, jax.numpy as jnp
from jax import lax
import numpy as np


D_MODEL = 1024
BATCH = 2
SEQ = 16384
DEPTH = 2

N_A_LAYERS = DEPTH // 2
N_B_LAYERS = DEPTH - N_A_LAYERS
D_FF = 2816
GM_HALF = D_MODEL
GM_GROUPS = 8
GM_GROUP_CH = GM_HALF // GM_GROUPS
CHUNK = 128
DA_HEADS = 8
DA_HEAD_DIM = D_MODEL // (2 * DA_HEADS)
DA_WIDTH = DA_HEADS * 2 * DA_HEAD_DIM
Q_BLOCK = 128
RMS_EPS = 1e-6
LN_EPS = 1e-5

kernel_name = "yoco_gmlp_diffattn_macaron"


def rms_norm(x, g):
    xf = x.astype(jnp.float32)
    y = xf * lax.rsqrt(jnp.mean(xf * xf, axis=-1, keepdims=True) + RMS_EPS)
    return (y * g.astype(jnp.float32)).astype(x.dtype)


def swiglu(h, w_gate, w_up, w_down):
    return (jax.nn.silu(h @ w_gate) * (h @ w_up)) @ w_down


def lambda_init_fn(layer_idx):
    return 0.8 - 0.6 * math.exp(-0.3 * layer_idx)


def chunked_gmlp(h, w_in, ln_g, ln_b, w_s, b_s, w_out):
    B, S, _ = h.shape
    z = jax.nn.gelu(h @ w_in, approximate=False)
    u, v = z[..., :GM_HALF], z[..., GM_HALF:]
    vf = v.astype(jnp.float32)
    mu = jnp.mean(vf, axis=-1, keepdims=True)
    var = jnp.mean(jnp.square(vf - mu), axis=-1, keepdims=True)
    v = ((vf - mu) * lax.rsqrt(var + LN_EPS) * ln_g.astype(jnp.float32)
         + ln_b.astype(jnp.float32)).astype(h.dtype)
    causal = jnp.tril(jnp.ones((CHUNK, CHUNK), dtype=bool))
    ws = jnp.where(causal[None], w_s, jnp.zeros_like(w_s))
    vc = v.reshape(B, S // CHUNK, CHUNK, GM_GROUPS, GM_GROUP_CH)
    sv = jnp.einsum('gts,bnsgc->bntgc', ws, vc) + b_s.T[None, None, :, :, None]
    return (u * sv.reshape(B, S, GM_HALF)) @ w_out


def shared_kv(x, g, w_k, w_v):
    h = rms_norm(x, g)
    B, S, _ = h.shape
    k = (h @ w_k).reshape(B, S, DA_HEADS, 2, DA_HEAD_DIM).transpose(0, 2, 3, 1, 4)
    v = (h @ w_v).reshape(B, S, DA_HEADS, 2 * DA_HEAD_DIM).transpose(0, 2, 1, 3)
    return k, v


def diff_attention(h, k_sh, v_sh, w_q, lam_q1, lam_k1, lam_q2, lam_k2, subln_g, w_o, lambda_init):
    B, S, _ = h.shape
    nb = S // Q_BLOCK
    q = (h @ w_q).reshape(B, nb, Q_BLOCK, DA_HEADS, 2, DA_HEAD_DIM)
    q_blocks = q.transpose(1, 0, 3, 4, 2, 5)
    lam = (jnp.exp(jnp.sum(lam_q1.astype(jnp.float32) * lam_k1.astype(jnp.float32)))
           - jnp.exp(jnp.sum(lam_q2.astype(jnp.float32) * lam_k2.astype(jnp.float32)))
           + lambda_init)
    slopes = jnp.exp2(-8.0 * (jnp.arange(DA_HEADS, dtype=jnp.float32) + 1.0) / DA_HEADS)
    key_pos = jnp.arange(S, dtype=jnp.int32)
    scale = DA_HEAD_DIM ** -0.5

    def one_block(args):
        qb, start = args
        s = jnp.einsum('bhiqd,bhikd->bhiqk', qb, k_sh).astype(jnp.float32) * scale
        dist = (start + jnp.arange(Q_BLOCK, dtype=jnp.int32))[:, None] - key_pos[None, :]
        s = s - slopes[None, :, None, None, None] * dist.astype(jnp.float32)
        s = jnp.where(dist >= 0, s, -jnp.inf)
        p = jax.nn.softmax(s, axis=-1)
        a = p[:, :, 0] - lam * p[:, :, 1]
        return jnp.einsum('bhqk,bhkd->bhqd', a.astype(v_sh.dtype), v_sh)

    starts = jnp.arange(nb, dtype=jnp.int32) * Q_BLOCK
    o = lax.map(one_block, (q_blocks, starts))
    o = o.transpose(1, 0, 3, 2, 4).reshape(B, S, DA_HEADS, 2 * DA_HEAD_DIM)
    o = rms_norm(o, subln_g) * (1.0 - lambda_init)
    return o.reshape(B, S, DA_WIDTH) @ w_o


def setup_inputs(seed: int = 0) -> dict:
    key = jax.random.key(seed)
    ks = iter(jax.random.split(key, 40))

    def nrm(shape, scale):
        return jax.random.normal(next(ks), shape, jnp.float32) * scale

    def gain(shape):
        return 1.0 + nrm(shape, 0.02)

    D, F = D_MODEL, D_FF
    return {
        "x": nrm((BATCH, SEQ, D), 1.0),
        "ffn_norm1_g": gain((DEPTH, D)),
        "ffn1_w_gate": nrm((DEPTH, D, F), D ** -0.5),
        "ffn1_w_up": nrm((DEPTH, D, F), D ** -0.5),
        "ffn1_w_down": nrm((DEPTH, F, D), F ** -0.5),
        "mix_norm_g": gain((DEPTH, D)),
        "ffn_norm2_g": gain((DEPTH, D)),
        "ffn2_w_gate": nrm((DEPTH, D, F), D ** -0.5),
        "ffn2_w_up": nrm((DEPTH, D, F), D ** -0.5),
        "ffn2_w_down": nrm((DEPTH, F, D), F ** -0.5),
        "gm_w_in": nrm((N_A_LAYERS, D, 2 * GM_HALF), D ** -0.5),
        "gm_ln_g": gain((N_A_LAYERS, GM_HALF)),
        "gm_ln_b": nrm((N_A_LAYERS, GM_HALF), 0.02),
        "gm_w_s": nrm((N_A_LAYERS, GM_GROUPS, CHUNK, CHUNK), CHUNK ** -0.5),
        "gm_b_s": gain((N_A_LAYERS, GM_GROUPS, CHUNK)),
        "gm_w_out": nrm((N_A_LAYERS, GM_HALF, D), GM_HALF ** -0.5),
        "kv_norm_g": gain((D,)),
        "w_k": nrm((D, DA_WIDTH), D ** -0.5),
        "w_v": nrm((D, DA_WIDTH), D ** -0.5),
        "da_w_q": nrm((N_B_LAYERS, D, DA_WIDTH), D ** -0.5),
        "da_lam_q1": nrm((N_B_LAYERS, DA_HEAD_DIM), 0.1),
        "da_lam_k1": nrm((N_B_LAYERS, DA_HEAD_DIM), 0.1),
        "da_lam_q2": nrm((N_B_LAYERS, DA_HEAD_DIM), 0.1),
        "da_lam_k2": nrm((N_B_LAYERS, DA_HEAD_DIM), 0.1),
        "da_subln_g": gain((N_B_LAYERS, 2 * DA_HEAD_DIM)),
        "da_w_o": nrm((N_B_LAYERS, DA_WIDTH, D), DA_WIDTH ** -0.5),
        "final_norm_g": gain((D,)),
    }


def reference(x, ffn_norm1_g, ffn1_w_gate, ffn1_w_up, ffn1_w_down, mix_norm_g,
              ffn_norm2_g, ffn2_w_gate, ffn2_w_up, ffn2_w_down,
              gm_w_in, gm_ln_g, gm_ln_b, gm_w_s, gm_b_s, gm_w_out,
              kv_norm_g, w_k, w_v,
              da_w_q, da_lam_q1, da_lam_k1, da_lam_q2, da_lam_k2, da_subln_g, da_w_o,
              final_norm_g):
    k_sh = None
    v_sh = None
    for l in range(DEPTH):
        if l == N_A_LAYERS:
            k_sh, v_sh = shared_kv(x, kv_norm_g, w_k, w_v)
        x = x + 0.5 * swiglu(rms_norm(x, ffn_norm1_g[l]), ffn1_w_gate[l], ffn1_w_up[l], ffn1_w_down[l])
        h = rms_norm(x, mix_norm_g[l])
        if l < N_A_LAYERS:
            x = x + chunked_gmlp(h, gm_w_in[l], gm_ln_g[l], gm_ln_b[l], gm_w_s[l], gm_b_s[l], gm_w_out[l])
        else:
            j = l - N_A_LAYERS
            x = x + diff_attention(h, k_sh, v_sh, da_w_q[j], da_lam_q1[j], da_lam_k1[j],
                                   da_lam_q2[j], da_lam_k2[j], da_subln_g[j], da_w_o[j],
                                   lambda_init_fn(l))
        x = x + 0.5 * swiglu(rms_norm(x, ffn_norm2_g[l]), ffn2_w_gate[l], ffn2_w_up[l], ffn2_w_down[l])
    return rms_norm(x, final_norm_g)
```

```python
import functools
import math

import jax
import jax.numpy as jnp
from jax import lax
from jax.experimental import pallas as pl
from jax.experimental.pallas import tpu as pltpu

D_MODEL = 1024
D_FF = 2816
GM_HALF = 1024
GM_GROUPS = 8
CHUNK = 128
DA_HEADS = 8
DA_HEAD_DIM = 64
HEAD_W = 2 * DA_HEAD_DIM
RMS_EPS = 1e-6
LN_EPS = 1e-5
N_A_LAYERS = 1

VMEM_LIMIT_BYTES = 56 * 1024 * 1024

FFN_ROWS = 512
FFN_CHUNK = 256
ATTN_TQ = 512
ATTN_TK = 512
MASK_VALUE = -1e30

BF16 = jnp.bfloat16
F32 = jnp.float32


def _params(n_grid):
    return pltpu.CompilerParams(
        dimension_semantics=("arbitrary",) * n_grid,
        vmem_limit_bytes=VMEM_LIMIT_BYTES)


def _resident(shape):
    nd = len(shape)
    return pl.BlockSpec(shape, lambda *_: (0,) * nd, pipeline_mode=pl.Buffered(1))


def _rms(x, g):
    return x * lax.rsqrt(jnp.mean(x * x, axis=-1, keepdims=True) + RMS_EPS) * g


def _ffn_kernel(x_ref, g_ref, wg_ref, wu_ref, wd_ref, *rest, n_chunks, final_norm):
    if final_norm:
        fg_ref, o_ref, h_ref, acc_ref = rest
    else:
        o_ref, h_ref, acc_ref = rest
    h_ref[...] = _rms(x_ref[...], g_ref[...]).astype(BF16)
    for c in range(n_chunks):
        h = h_ref[...]
        gate = jnp.dot(h, wg_ref[c], preferred_element_type=F32)
        up = jnp.dot(h, wu_ref[c], preferred_element_type=F32)
        a = (gate * jax.nn.sigmoid(gate) * up).astype(BF16)
        part = jnp.dot(a, wd_ref[c], preferred_element_type=F32)
        if c == 0:
            acc_ref[...] = part
        else:
            acc_ref[...] += part
    y = x_ref[...] + 0.5 * acc_ref[...]
    if final_norm:
        y = _rms(y, fg_ref[...])
    o_ref[...] = y


def _ffn(x, g, wg, wu, wd, final_g=None):
    t, d = x.shape
    n_chunks = wg.shape[0]
    row = pl.BlockSpec((FFN_ROWS, d), lambda i: (i, 0))
    in_specs = [row, _resident((1, d)), _resident(wg.shape), _resident(wu.shape), _resident(wd.shape)]
    args = [x, g.reshape(1, d), wg, wu, wd]
    if final_g is not None:
        in_specs.append(_resident((1, d)))
        args.append(final_g.reshape(1, d))
    return pl.pallas_call(
        functools.partial(_ffn_kernel, n_chunks=n_chunks, final_norm=final_g is not None),
        grid=(t // FFN_ROWS,),
        in_specs=in_specs,
        out_specs=row,
        out_shape=jax.ShapeDtypeStruct((t, d), F32),
        scratch_shapes=[pltpu.VMEM((FFN_ROWS, d), BF16), pltpu.VMEM((FFN_ROWS, d), F32)],
        compiler_params=_params(1),
        name="ffn_final" if final_g is not None else "ffn",
    )(*args)


def _ffn_weights(w_gate, w_up, w_down):
    d, f = w_gate.shape
    n = f // FFN_CHUNK
    wg = w_gate.astype(BF16).reshape(d, n, FFN_CHUNK).transpose(1, 0, 2)
    wu = w_up.astype(BF16).reshape(d, n, FFN_CHUNK).transpose(1, 0, 2)
    wd = w_down.astype(BF16).reshape(n, FFN_CHUNK, d)
    return wg, wu, wd


def _gelu(z):
    return 0.5 * z * (1.0 + lax.erf(z * (1.0 / math.sqrt(2.0))))


def _gmlp_kernel(x_ref, g_ref, win_ref, lng_ref, lnb_ref, ws_ref, bs_ref, wout_ref, o_ref,
                 h_ref, v_ref, y_ref, *, rows):
    h_ref[...] = _rms(x_ref[...], g_ref[...]).astype(BF16)
    v = _gelu(jnp.dot(h_ref[...], win_ref[:, GM_HALF:], preferred_element_type=F32))
    mu = jnp.mean(v, axis=-1, keepdims=True)
    vc = v - mu
    var = jnp.mean(vc * vc, axis=-1, keepdims=True)
    v_ref[...] = (vc * lax.rsqrt(var + LN_EPS) * lng_ref[...] + lnb_ref[...]).astype(BF16)
    u = _gelu(jnp.dot(h_ref[...], win_ref[:, :GM_HALF], preferred_element_type=F32))
    for n in range(rows // CHUNK):
        r = slice(n * CHUNK, (n + 1) * CHUNK)
        for grp in range(GM_GROUPS):
            c = slice(grp * CHUNK, (grp + 1) * CHUNK)
            sv = jnp.dot(ws_ref[grp], v_ref[r, c], preferred_element_type=F32) + bs_ref[grp]
            y_ref[r, c] = (u[r, c] * sv).astype(BF16)
    o_ref[...] = x_ref[...] + jnp.dot(y_ref[...], wout_ref[...], preferred_element_type=F32)


def _gmlp(x, g, w_in, ln_g, ln_b, w_s, b_s, w_out, rows=256):
    t, d = x.shape
    row = pl.BlockSpec((rows, d), lambda i: (i, 0))
    causal = jnp.tril(jnp.ones((CHUNK, CHUNK), dtype=bool))
    ws = jnp.where(causal[None], w_s, jnp.zeros_like(w_s)).astype(BF16)
    bs = jnp.broadcast_to(b_s[:, :, None], (GM_GROUPS, CHUNK, CHUNK)).astype(F32)
    args = [x, g.reshape(1, d), w_in.astype(BF16), ln_g.reshape(1, GM_HALF), ln_b.reshape(1, GM_HALF),
            ws, bs, w_out.astype(BF16)]
    in_specs = [row] + [_resident(a.shape) for a in args[1:]]
    return pl.pallas_call(
        functools.partial(_gmlp_kernel, rows=rows),
        grid=(t // rows,),
        in_specs=in_specs,
        out_specs=row,
        out_shape=jax.ShapeDtypeStruct((t, d), F32),
        scratch_shapes=[pltpu.VMEM((rows, d), BF16), pltpu.VMEM((rows, GM_HALF), BF16),
                        pltpu.VMEM((rows, GM_HALF), BF16)],
        compiler_params=_params(1),
        name="gmlp",
    )(*args)


def _proj_kernel(x_ref, g_ref, w_ref, o_ref):
    h = _rms(x_ref[...], g_ref[...]).astype(BF16)
    o_ref[...] = jnp.dot(h, w_ref[...], preferred_element_type=F32).astype(BF16)


def _proj(x, g, w):
    t, d = x.shape
    n = w.shape[1]
    return pl.pallas_call(
        _proj_kernel,
        grid=(t // FFN_ROWS,),
        in_specs=[pl.BlockSpec((FFN_ROWS, d), lambda i: (i, 0)), _resident((1, d)), _resident(w.shape)],
        out_specs=pl.BlockSpec((FFN_ROWS, n), lambda i: (i, 0)),
        out_shape=jax.ShapeDtypeStruct((t, n), BF16),
        compiler_params=_params(1),
        name="proj",
    )(x, g.reshape(1, d), w)


def _attn_kernel(slopes_ref, q_ref, k_ref, v_ref, lq1_ref, lk1_ref, lq2_ref, lk2_ref, subg_ref, o_ref,
                 qp_ref, m_ref, l_ref, acc_ref, bias_ref, *, lambda_init):
    hd = pl.program_id(1)
    qi = pl.program_id(2)
    slope = slopes_ref[hd]

    @pl.when(qi == 0)
    def _():
        rel = (lax.broadcasted_iota(jnp.int32, (ATTN_TQ, ATTN_TK), 0)
               - lax.broadcasted_iota(jnp.int32, (ATTN_TQ, ATTN_TK), 1))
        alibi = -slope * rel.astype(F32)
        bias_ref[0] = alibi
        bias_ref[1] = jnp.where(rel >= 0, alibi, MASK_VALUE)

    q = q_ref[...]
    lane = lax.broadcasted_iota(jnp.int32, q.shape, 1)
    qp_ref[0] = jnp.where(lane < DA_HEAD_DIM, q, jnp.zeros_like(q))
    qp_ref[1] = jnp.where(lane >= DA_HEAD_DIM, q, jnp.zeros_like(q))
    m_ref[...] = jnp.full(m_ref.shape, MASK_VALUE, F32)
    l_ref[...] = jnp.zeros(l_ref.shape, F32)
    acc_ref[...] = jnp.zeros(acc_ref.shape, F32)

    def block(kj, bias, shift):
        start = pl.multiple_of(kj * ATTN_TK, ATTN_TK)
        k = k_ref[pl.ds(start, ATTN_TK), :]
        v = v_ref[pl.ds(start, ATTN_TK), :]
        for i in range(2):
            s = lax.dot_general(qp_ref[i], k, (((1,), (1,)), ((), ())), preferred_element_type=F32) + bias
            m_old = m_ref[i]
            m_new = jnp.maximum(m_old, jnp.max(s, axis=-1, keepdims=True) + shift)
            alpha = jnp.exp(m_old - m_new)
            p = jnp.exp(s - (m_new - shift))
            l_ref[i] = alpha * l_ref[i] + jnp.sum(p, axis=-1, keepdims=True)
            acc_ref[i] = alpha * acc_ref[i] + jnp.dot(p.astype(BF16), v, preferred_element_type=F32)
            m_ref[i] = m_new

    def below_diagonal(kj, carry):
        block(kj, bias_ref[0], -slope * ((qi - kj) * ATTN_TQ).astype(F32))
        return carry

    lax.fori_loop(0, qi, below_diagonal, 0)
    block(qi, bias_ref[1], 0.0)

    lam = (jnp.exp(jnp.sum(lq1_ref[...] * lk1_ref[...], axis=-1, keepdims=True))
           - jnp.exp(jnp.sum(lq2_ref[...] * lk2_ref[...], axis=-1, keepdims=True))
           + lambda_init)
    o = acc_ref[0] / l_ref[0] - lam * (acc_ref[1] / l_ref[1])
    o_ref[...] = (_rms(o, subg_ref[...]) * (1.0 - lambda_init)).astype(BF16)


def _attention(q, kv, slopes, lam_q1, lam_k1, lam_q2, lam_k2, subln_g, batch, seq, lambda_init):
    nq = seq // ATTN_TQ
    vec = lambda a: a.reshape(1, -1).astype(F32)
    lam_spec = _resident((1, DA_HEAD_DIM))
    return pl.pallas_call(
        functools.partial(_attn_kernel, lambda_init=lambda_init),
        grid=(batch, DA_HEADS, nq),
        in_specs=[
            pl.BlockSpec(memory_space=pltpu.SMEM),
            pl.BlockSpec((None, ATTN_TQ, HEAD_W), lambda b, h, i: (b, i, h)),
            pl.BlockSpec((None, seq, HEAD_W), lambda b, h, i: (b, 0, h)),
            pl.BlockSpec((None, seq, HEAD_W), lambda b, h, i: (b, 0, DA_HEADS + h)),
            lam_spec, lam_spec, lam_spec, lam_spec,
            _resident((1, HEAD_W)),
        ],
        out_specs=pl.BlockSpec((None, ATTN_TQ, HEAD_W), lambda b, h, i: (b, i, h)),
        out_shape=jax.ShapeDtypeStruct((batch, seq, DA_HEADS * HEAD_W), BF16),
        scratch_shapes=[
            pltpu.VMEM((2, ATTN_TQ, HEAD_W), BF16),
            pltpu.VMEM((2, ATTN_TQ, 1), F32),
            pltpu.VMEM((2, ATTN_TQ, 1), F32),
            pltpu.VMEM((2, ATTN_TQ, HEAD_W), F32),
            pltpu.VMEM((2, ATTN_TQ, ATTN_TK), F32),
        ],
        compiler_params=_params(3),
        name="diff_attn",
    )(slopes, q, kv, kv, vec(lam_q1), vec(lam_k1), vec(lam_q2), vec(lam_k2), vec(subln_g))


def _outproj_kernel(x_ref, a_ref, w_ref, o_ref):
    o_ref[...] = x_ref[...] + jnp.dot(a_ref[...], w_ref[...], preferred_element_type=F32)


def _outproj(x, a, w):
    t, d = x.shape
    row = pl.BlockSpec((FFN_ROWS, d), lambda i: (i, 0))
    return pl.pallas_call(
        _outproj_kernel,
        grid=(t // FFN_ROWS,),
        in_specs=[row, pl.BlockSpec((FFN_ROWS, a.shape[1]), lambda i: (i, 0)), _resident(w.shape)],
        out_specs=row,
        out_shape=jax.ShapeDtypeStruct((t, d), F32),
        compiler_params=_params(1),
        name="outproj",
    )(x, a, w)


def kernel(x, ffn_norm1_g, ffn1_w_gate, ffn1_w_up, ffn1_w_down, mix_norm_g, ffn_norm2_g, ffn2_w_gate, ffn2_w_up, ffn2_w_down, gm_w_in, gm_ln_g, gm_ln_b, gm_w_s, gm_b_s, gm_w_out, kv_norm_g, w_k, w_v, da_w_q, da_lam_q1, da_lam_k1, da_lam_q2, da_lam_k2, da_subln_g, da_w_o, final_norm_g):
    batch, seq, d = x.shape
    depth = ffn_norm1_g.shape[0]
    assert (d, depth, N_A_LAYERS) == (D_MODEL, 2, 1)
    xs = x.reshape(batch * seq, d)
    kv = None
    for l in range(depth):
        if l == N_A_LAYERS:
            w_kv = jnp.concatenate([w_k, w_v], axis=1).astype(BF16)
            kv = _proj(xs, kv_norm_g, w_kv)
        xs = _ffn(xs, ffn_norm1_g[l], *_ffn_weights(ffn1_w_gate[l], ffn1_w_up[l], ffn1_w_down[l]))
        if l < N_A_LAYERS:
            xs = _gmlp(xs, mix_norm_g[l], gm_w_in[l], gm_ln_g[l], gm_ln_b[l], gm_w_s[l], gm_b_s[l], gm_w_out[l])
        else:
            j = l - N_A_LAYERS
            lambda_init = 0.8 - 0.6 * math.exp(-0.3 * l)
            w_q = (da_w_q[j] * (DA_HEAD_DIM ** -0.5)).astype(BF16)
            q = _proj(xs, mix_norm_g[l], w_q)
            slopes = jnp.exp2(-8.0 * (jnp.arange(DA_HEADS, dtype=F32) + 1.0) / DA_HEADS)
            a = _attention(q.reshape(batch, seq, -1), kv.reshape(batch, seq, -1), slopes,
                           da_lam_q1[j], da_lam_k1[j], da_lam_q2[j], da_lam_k2[j], da_subln_g[j],
                           batch, seq, lambda_init)
            xs = _outproj(xs, a.reshape(batch * seq, -1), da_w_o[j].astype(BF16))
        final_g = final_norm_g if l == depth - 1 else None
        xs = _ffn(xs, ffn_norm2_g[l], *_ffn_weights(ffn2_w_gate[l], ffn2_w_up[l], ffn2_w_down[l]), final_g=final_g)
    return xs.reshape(batch, seq, d)
```

```python
import functools
import math

import jax
import jax.numpy as jnp
from jax import lax
from jax.experimental import pallas as pl
from jax.experimental.pallas import tpu as pltpu

D_MODEL = 1024
D_FF = 2816
GM_HALF = 1024
GM_GROUPS = 8
CHUNK = 128
DA_HEADS = 8
DA_HEAD_DIM = 64
HEAD_W = 2 * DA_HEAD_DIM
RMS_EPS = 1e-6
LN_EPS = 1e-5
N_A_LAYERS = 1

VMEM_LIMIT_BYTES = 56 * 1024 * 1024

FFN_ROWS = 512
FFN_CHUNK = 256
ATTN_TQ = 512
ATTN_TK = 512
MASK_VALUE = -1e30

BF16 = jnp.bfloat16
F32 = jnp.float32


def _params(n_grid):
    return pltpu.CompilerParams(
        dimension_semantics=("arbitrary",) * n_grid,
        vmem_limit_bytes=VMEM_LIMIT_BYTES)


def _resident(shape):
    nd = len(shape)
    return pl.BlockSpec(shape, lambda *_: (0,) * nd, pipeline_mode=pl.Buffered(1))


def _rms(x, g):
    return x * lax.rsqrt(jnp.mean(x * x, axis=-1, keepdims=True) + RMS_EPS) * g


def _ffn_kernel(x_ref, g_ref, wg_ref, wu_ref, wd_ref, *rest, n_chunks, final_norm):
    if final_norm:
        fg_ref, o_ref, h_ref, acc_ref = rest
    else:
        o_ref, h_ref, acc_ref = rest
    h_ref[...] = _rms(x_ref[...], g_ref[...]).astype(BF16)
    for c in range(n_chunks):
        h = h_ref[...]
        gate = jnp.dot(h, wg_ref[c], preferred_element_type=F32)
        up = jnp.dot(h, wu_ref[c], preferred_element_type=F32)
        a = (gate * jax.nn.sigmoid(gate) * up).astype(BF16)
        part = jnp.dot(a, wd_ref[c], preferred_element_type=F32)
        if c == 0:
            acc_ref[...] = part
        else:
            acc_ref[...] += part
    y = x_ref[...] + 0.5 * acc_ref[...]
    if final_norm:
        y = _rms(y, fg_ref[...])
    o_ref[...] = y


def _ffn(x, g, wg, wu, wd, final_g=None):
    t, d = x.shape
    n_chunks = wg.shape[0]
    row = pl.BlockSpec((FFN_ROWS, d), lambda i: (i, 0))
    in_specs = [row, _resident((1, d)), _resident(wg.shape), _resident(wu.shape), _resident(wd.shape)]
    args = [x, g.reshape(1, d), wg, wu, wd]
    if final_g is not None:
        in_specs.append(_resident((1, d)))
        args.append(final_g.reshape(1, d))
    return pl.pallas_call(
        functools.partial(_ffn_kernel, n_chunks=n_chunks, final_norm=final_g is not None),
        grid=(t // FFN_ROWS,),
        in_specs=in_specs,
        out_specs=row,
        out_shape=jax.ShapeDtypeStruct((t, d), F32),
        scratch_shapes=[pltpu.VMEM((FFN_ROWS, d), BF16), pltpu.VMEM((FFN_ROWS, d), F32)],
        compiler_params=_params(1),
        name="ffn_final" if final_g is not None else "ffn",
    )(*args)


def _ffn_weights(w_gate, w_up, w_down):
    d, f = w_gate.shape
    n = f // FFN_CHUNK
    wg = w_gate.astype(BF16).reshape(d, n, FFN_CHUNK).transpose(1, 0, 2)
    wu = w_up.astype(BF16).reshape(d, n, FFN_CHUNK).transpose(1, 0, 2)
    wd = w_down.astype(BF16).reshape(n, FFN_CHUNK, d)
    return wg, wu, wd


def _gelu(z):
    return 0.5 * z * (1.0 + lax.erf(z * (1.0 / math.sqrt(2.0))))


def _gmlp_kernel(x_ref, g_ref, win_ref, lng_ref, lnb_ref, ws_ref, bs_ref, wout_ref, o_ref,
                 h_ref, v_ref, y_ref, *, rows):
    h_ref[...] = _rms(x_ref[...], g_ref[...]).astype(BF16)
    v = _gelu(jnp.dot(h_ref[...], win_ref[:, GM_HALF:], preferred_element_type=F32))
    mu = jnp.mean(v, axis=-1, keepdims=True)
    vc = v - mu
    var = jnp.mean(vc * vc, axis=-1, keepdims=True)
    v_ref[...] = (vc * lax.rsqrt(var + LN_EPS) * lng_ref[...] + lnb_ref[...]).astype(BF16)
    u = _gelu(jnp.dot(h_ref[...], win_ref[:, :GM_HALF], preferred_element_type=F32))
    for n in range(rows // CHUNK):
        r = slice(n * CHUNK, (n + 1) * CHUNK)
        for grp in range(GM_GROUPS):
            c = slice(grp * CHUNK, (grp + 1) * CHUNK)
            sv = jnp.dot(ws_ref[grp], v_ref[r, c], preferred_element_type=F32) + bs_ref[grp]
            y_ref[r, c] = (u[r, c] * sv).astype(BF16)
    o_ref[...] = x_ref[...] + jnp.dot(y_ref[...], wout_ref[...], preferred_element_type=F32)


def _gmlp(x, g, w_in, ln_g, ln_b, w_s, b_s, w_out, rows=256):
    t, d = x.shape
    row = pl.BlockSpec((rows, d), lambda i: (i, 0))
    causal = jnp.tril(jnp.ones((CHUNK, CHUNK), dtype=bool))
    ws = jnp.where(causal[None], w_s, jnp.zeros_like(w_s)).astype(BF16)
    bs = jnp.broadcast_to(b_s[:, :, None], (GM_GROUPS, CHUNK, CHUNK)).astype(F32)
    args = [x, g.reshape(1, d), w_in.astype(BF16), ln_g.reshape(1, GM_HALF), ln_b.reshape(1, GM_HALF),
            ws, bs, w_out.astype(BF16)]
    in_specs = [row] + [_resident(a.shape) for a in args[1:]]
    return pl.pallas_call(
        functools.partial(_gmlp_kernel, rows=rows),
        grid=(t // rows,),
        in_specs=in_specs,
        out_specs=row,
        out_shape=jax.ShapeDtypeStruct((t, d), F32),
        scratch_shapes=[pltpu.VMEM((rows, d), BF16), pltpu.VMEM((rows, GM_HALF), BF16),
                        pltpu.VMEM((rows, GM_HALF), BF16)],
        compiler_params=_params(1),
        name="gmlp",
    )(*args)


def _proj_kernel(x_ref, g_ref, w_ref, o_ref):
    h = _rms(x_ref[...], g_ref[...]).astype(BF16)
    o_ref[...] = jnp.dot(h, w_ref[...], preferred_element_type=F32).astype(BF16)


def _proj(x, g, w):
    t, d = x.shape
    n = w.shape[1]
    return pl.pallas_call(
        _proj_kernel,
        grid=(t // FFN_ROWS,),
        in_specs=[pl.BlockSpec((FFN_ROWS, d), lambda i: (i, 0)), _resident((1, d)), _resident(w.shape)],
        out_specs=pl.BlockSpec((FFN_ROWS, n), lambda i: (i, 0)),
        out_shape=jax.ShapeDtypeStruct((t, n), BF16),
        compiler_params=_params(1),
        name="proj",
    )(x, g.reshape(1, d), w)


def _attn_kernel(slopes_ref, q_ref, k_ref, v_ref, lq1_ref, lk1_ref, lq2_ref, lk2_ref, subg_ref, o_ref,
                 qp_ref, m_ref, l_ref, acc_ref, bias_ref, *, lambda_init):
    hd = pl.program_id(1)
    qi = pl.program_id(2)
    slope = slopes_ref[hd]

    @pl.when(qi == 0)
    def _():
        rel = (lax.broadcasted_iota(jnp.int32, (ATTN_TQ, ATTN_TK), 0)
               - lax.broadcasted_iota(jnp.int32, (ATTN_TQ, ATTN_TK), 1))
        alibi = -slope * rel.astype(F32)
        bias_ref[0] = alibi
        bias_ref[1] = jnp.where(rel >= 0, alibi, MASK_VALUE)

    q = q_ref[...]
    lane = lax.broadcasted_iota(jnp.int32, q.shape, 1)
    qp_ref[0] = jnp.where(lane < DA_HEAD_DIM, q, jnp.zeros_like(q))
    qp_ref[1] = jnp.where(lane >= DA_HEAD_DIM, q, jnp.zeros_like(q))
    m_ref[...] = jnp.full(m_ref.shape, MASK_VALUE, F32)
    l_ref[...] = jnp.zeros(l_ref.shape, F32)
    acc_ref[...] = jnp.zeros(acc_ref.shape, F32)

    def block(kj, bias, shift):
        start = pl.multiple_of(kj * ATTN_TK, ATTN_TK)
        k = k_ref[pl.ds(start, ATTN_TK), :]
        v = v_ref[pl.ds(start, ATTN_TK), :]
        for i in range(2):
            s = lax.dot_general(qp_ref[i], k, (((1,), (1,)), ((), ())), preferred_element_type=F32) + bias
            m_old = m_ref[i]
            m_new = jnp.maximum(m_old, jnp.max(s, axis=-1, keepdims=True) + shift)
            alpha = jnp.exp(m_old - m_new)
            p = jnp.exp(s - pltpu.repeat(m_new - shift, ATTN_TK // HEAD_W, axis=1))
            l_ref[i] = alpha * l_ref[i] + jnp.sum(p, axis=-1, keepdims=True)
            acc_ref[i] = alpha * acc_ref[i] + jnp.dot(p.astype(BF16), v, preferred_element_type=F32)
            m_ref[i] = m_new

    def below_diagonal(kj, carry):
        block(kj, bias_ref[0], -slope * ((qi - kj) * ATTN_TQ).astype(F32))
        return carry

    lax.fori_loop(0, qi, below_diagonal, 0)
    block(qi, bias_ref[1], 0.0)

    lam = (jnp.exp(jnp.sum(lq1_ref[...] * lk1_ref[...], axis=-1, keepdims=True))
           - jnp.exp(jnp.sum(lq2_ref[...] * lk2_ref[...], axis=-1, keepdims=True))
           + lambda_init)
    o = acc_ref[0] / l_ref[0] - lam * (acc_ref[1] / l_ref[1])
    o_ref[...] = (_rms(o, subg_ref[...]) * (1.0 - lambda_init)).astype(BF16)


def _attention(q, kv, slopes, lam_q1, lam_k1, lam_q2, lam_k2, subln_g, batch, seq, lambda_init):
    nq = seq // ATTN_TQ
    vec = lambda a: a.reshape(1, -1).astype(F32)
    lam_spec = _resident((1, DA_HEAD_DIM))
    return pl.pallas_call(
        functools.partial(_attn_kernel, lambda_init=lambda_init),
        grid=(batch, DA_HEADS, nq),
        in_specs=[
            pl.BlockSpec(memory_space=pltpu.SMEM),
            pl.BlockSpec((None, ATTN_TQ, HEAD_W), lambda b, h, i: (b, i, h)),
            pl.BlockSpec((None, seq, HEAD_W), lambda b, h, i: (b, 0, h)),
            pl.BlockSpec((None, seq, HEAD_W), lambda b, h, i: (b, 0, DA_HEADS + h)),
            lam_spec, lam_spec, lam_spec, lam_spec,
            _resident((1, HEAD_W)),
        ],
        out_specs=pl.BlockSpec((None, ATTN_TQ, HEAD_W), lambda b, h, i: (b, i, h)),
        out_shape=jax.ShapeDtypeStruct((batch, seq, DA_HEADS * HEAD_W), BF16),
        scratch_shapes=[
            pltpu.VMEM((2, ATTN_TQ, HEAD_W), BF16),
            pltpu.VMEM((2, ATTN_TQ, HEAD_W), F32),
            pltpu.VMEM((2, ATTN_TQ, HEAD_W), F32),
            pltpu.VMEM((2, ATTN_TQ, HEAD_W), F32),
            pltpu.VMEM((2, ATTN_TQ, ATTN_TK), F32),
        ],
        compiler_params=_params(3),
        name="diff_attn",
    )(slopes, q, kv, kv, vec(lam_q1), vec(lam_k1), vec(lam_q2), vec(lam_k2), vec(subln_g))


def _outproj_kernel(x_ref, a_ref, w_ref, o_ref):
    o_ref[...] = x_ref[...] + jnp.dot(a_ref[...], w_ref[...], preferred_element_type=F32)


def _outproj(x, a, w):
    t, d = x.shape
    row = pl.BlockSpec((FFN_ROWS, d), lambda i: (i, 0))
    return pl.pallas_call(
        _outproj_kernel,
        grid=(t // FFN_ROWS,),
        in_specs=[row, pl.BlockSpec((FFN_ROWS, a.shape[1]), lambda i: (i, 0)), _resident(w.shape)],
        out_specs=row,
        out_shape=jax.ShapeDtypeStruct((t, d), F32),
        compiler_params=_params(1),
        name="outproj",
    )(x, a, w)


def kernel(x, ffn_norm1_g, ffn1_w_gate, ffn1_w_up, ffn1_w_down, mix_norm_g, ffn_norm2_g, ffn2_w_gate, ffn2_w_up, ffn2_w_down, gm_w_in, gm_ln_g, gm_ln_b, gm_w_s, gm_b_s, gm_w_out, kv_norm_g, w_k, w_v, da_w_q, da_lam_q1, da_lam_k1, da_lam_q2, da_lam_k2, da_subln_g, da_w_o, final_norm_g):
    batch, seq, d = x.shape
    depth = ffn_norm1_g.shape[0]
    assert (d, depth, N_A_LAYERS) == (D_MODEL, 2, 1)
    xs = x.reshape(batch * seq, d)
    kv = None
    for l in range(depth):
        if l == N_A_LAYERS:
            w_kv = jnp.concatenate([w_k, w_v], axis=1).astype(BF16)
            kv = _proj(xs, kv_norm_g, w_kv)
        xs = _ffn(xs, ffn_norm1_g[l], *_ffn_weights(ffn1_w_gate[l], ffn1_w_up[l], ffn1_w_down[l]))
        if l < N_A_LAYERS:
            xs = _gmlp(xs, mix_norm_g[l], gm_w_in[l], gm_ln_g[l], gm_ln_b[l], gm_w_s[l], gm_b_s[l], gm_w_out[l])
        else:
            j = l - N_A_LAYERS
            lambda_init = 0.8 - 0.6 * math.exp(-0.3 * l)
            w_q = (da_w_q[j] * (DA_HEAD_DIM ** -0.5)).astype(BF16)
            q = _proj(xs, mix_norm_g[l], w_q)
            slopes = jnp.exp2(-8.0 * (jnp.arange(DA_HEADS, dtype=F32) + 1.0) / DA_HEADS)
            a = _attention(q.reshape(batch, seq, -1), kv.reshape(batch, seq, -1), slopes,
                           da_lam_q1[j], da_lam_k1[j], da_lam_q2[j], da_lam_k2[j], da_subln_g[j],
                           batch, seq, lambda_init)
            xs = _outproj(xs, a.reshape(batch * seq, -1), da_w_o[j].astype(BF16))
        final_g = final_norm_g if l == depth - 1 else None
        xs = _ffn(xs, ffn_norm2_g[l], *_ffn_weights(ffn2_w_gate[l], ffn2_w_up[l], ffn2_w_down[l]), final_g=final_g)
    return xs.reshape(batch, seq, d)
```

```python
import functools
import math

import jax
import jax.numpy as jnp
from jax import lax
from jax.experimental import pallas as pl
from jax.experimental.pallas import tpu as pltpu

D_MODEL = 1024
D_FF = 2816
GM_HALF = 1024
GM_GROUPS = 8
CHUNK = 128
DA_HEADS = 8
DA_HEAD_DIM = 64
HEAD_W = 2 * DA_HEAD_DIM
RMS_EPS = 1e-6
LN_EPS = 1e-5
N_A_LAYERS = 1

VMEM_LIMIT_BYTES = 56 * 1024 * 1024

FFN_ROWS = 512
FFN_CHUNK = 256
ATTN_TQ = 512
ATTN_TK = 512
MASK_VALUE = -1e30
LOG2_E = 1.4426950408889634

BF16 = jnp.bfloat16
F32 = jnp.float32


def _params(n_grid):
    return pltpu.CompilerParams(
        dimension_semantics=("arbitrary",) * n_grid,
        vmem_limit_bytes=VMEM_LIMIT_BYTES)


def _resident(shape):
    nd = len(shape)
    return pl.BlockSpec(shape, lambda *_: (0,) * nd, pipeline_mode=pl.Buffered(1))


def _rms(x, g):
    return x * lax.rsqrt(jnp.mean(x * x, axis=-1, keepdims=True) + RMS_EPS) * g


def _ffn_kernel(x_ref, g_ref, wg_ref, wu_ref, wd_ref, *rest, n_chunks, final_norm):
    if final_norm:
        fg_ref, o_ref, h_ref, acc_ref = rest
    else:
        o_ref, h_ref, acc_ref = rest
    h_ref[...] = _rms(x_ref[...], g_ref[...]).astype(BF16)
    for c in range(n_chunks):
        h = h_ref[...]
        gate = jnp.dot(h, wg_ref[c], preferred_element_type=F32)
        up = jnp.dot(h, wu_ref[c], preferred_element_type=F32)
        a = (gate * jax.nn.sigmoid(gate) * up).astype(BF16)
        part = jnp.dot(a, wd_ref[c], preferred_element_type=F32)
        if c == 0:
            acc_ref[...] = part
        else:
            acc_ref[...] += part
    y = x_ref[...] + 0.5 * acc_ref[...]
    if final_norm:
        y = _rms(y, fg_ref[...])
    o_ref[...] = y


def _ffn(x, g, wg, wu, wd, final_g=None):
    t, d = x.shape
    n_chunks = wg.shape[0]
    row = pl.BlockSpec((FFN_ROWS, d), lambda i: (i, 0))
    in_specs = [row, _resident((1, d)), _resident(wg.shape), _resident(wu.shape), _resident(wd.shape)]
    args = [x, g.reshape(1, d), wg, wu, wd]
    if final_g is not None:
        in_specs.append(_resident((1, d)))
        args.append(final_g.reshape(1, d))
    return pl.pallas_call(
        functools.partial(_ffn_kernel, n_chunks=n_chunks, final_norm=final_g is not None),
        grid=(t // FFN_ROWS,),
        in_specs=in_specs,
        out_specs=row,
        out_shape=jax.ShapeDtypeStruct((t, d), F32),
        scratch_shapes=[pltpu.VMEM((FFN_ROWS, d), BF16), pltpu.VMEM((FFN_ROWS, d), F32)],
        compiler_params=_params(1),
        name="ffn_final" if final_g is not None else "ffn",
    )(*args)


def _ffn_weights(w_gate, w_up, w_down):
    d, f = w_gate.shape
    n = f // FFN_CHUNK
    wg = w_gate.astype(BF16).reshape(d, n, FFN_CHUNK).transpose(1, 0, 2)
    wu = w_up.astype(BF16).reshape(d, n, FFN_CHUNK).transpose(1, 0, 2)
    wd = w_down.astype(BF16).reshape(n, FFN_CHUNK, d)
    return wg, wu, wd


def _gelu(z):
    return 0.5 * z * (1.0 + lax.erf(z * (1.0 / math.sqrt(2.0))))


def _gmlp_kernel(x_ref, g_ref, win_ref, lng_ref, lnb_ref, ws_ref, bs_ref, wout_ref, o_ref,
                 h_ref, v_ref, y_ref, *, rows):
    h_ref[...] = _rms(x_ref[...], g_ref[...]).astype(BF16)
    v = _gelu(jnp.dot(h_ref[...], win_ref[:, GM_HALF:], preferred_element_type=F32))
    mu = jnp.mean(v, axis=-1, keepdims=True)
    vc = v - mu
    var = jnp.mean(vc * vc, axis=-1, keepdims=True)
    v_ref[...] = (vc * lax.rsqrt(var + LN_EPS) * lng_ref[...] + lnb_ref[...]).astype(BF16)
    u = _gelu(jnp.dot(h_ref[...], win_ref[:, :GM_HALF], preferred_element_type=F32))
    for n in range(rows // CHUNK):
        r = slice(n * CHUNK, (n + 1) * CHUNK)
        for grp in range(GM_GROUPS):
            c = slice(grp * CHUNK, (grp + 1) * CHUNK)
            sv = jnp.dot(ws_ref[grp], v_ref[r, c], preferred_element_type=F32) + bs_ref[grp]
            y_ref[r, c] = (u[r, c] * sv).astype(BF16)
    o_ref[...] = x_ref[...] + jnp.dot(y_ref[...], wout_ref[...], preferred_element_type=F32)


def _gmlp(x, g, w_in, ln_g, ln_b, w_s, b_s, w_out, rows=256):
    t, d = x.shape
    row = pl.BlockSpec((rows, d), lambda i: (i, 0))
    causal = jnp.tril(jnp.ones((CHUNK, CHUNK), dtype=bool))
    ws = jnp.where(causal[None], w_s, jnp.zeros_like(w_s)).astype(BF16)
    bs = jnp.broadcast_to(b_s[:, :, None], (GM_GROUPS, CHUNK, CHUNK)).astype(F32)
    args = [x, g.reshape(1, d), w_in.astype(BF16), ln_g.reshape(1, GM_HALF), ln_b.reshape(1, GM_HALF),
            ws, bs, w_out.astype(BF16)]
    in_specs = [row] + [_resident(a.shape) for a in args[1:]]
    return pl.pallas_call(
        functools.partial(_gmlp_kernel, rows=rows),
        grid=(t // rows,),
        in_specs=in_specs,
        out_specs=row,
        out_shape=jax.ShapeDtypeStruct((t, d), F32),
        scratch_shapes=[pltpu.VMEM((rows, d), BF16), pltpu.VMEM((rows, GM_HALF), BF16),
                        pltpu.VMEM((rows, GM_HALF), BF16)],
        compiler_params=_params(1),
        name="gmlp",
    )(*args)


def _proj_kernel(x_ref, g_ref, w_ref, o_ref, *, scale):
    h = _rms(x_ref[...], g_ref[...]).astype(BF16)
    o_ref[...] = (jnp.dot(h, w_ref[...], preferred_element_type=F32) * scale).astype(BF16)


def _proj(x, g, w, scale=1.0):
    t, d = x.shape
    n = w.shape[1]
    return pl.pallas_call(
        functools.partial(_proj_kernel, scale=scale),
        grid=(t // FFN_ROWS,),
        in_specs=[pl.BlockSpec((FFN_ROWS, d), lambda i: (i, 0)), _resident((1, d)), _resident(w.shape)],
        out_specs=pl.BlockSpec((FFN_ROWS, n), lambda i: (i, 0)),
        out_shape=jax.ShapeDtypeStruct((t, n), BF16),
        compiler_params=_params(1),
        name="proj",
    )(x, g.reshape(1, d), w)


def _split_bf16(x, parts):
    out = []
    for _ in range(parts):
        t = x.astype(BF16).astype(F32)
        out.append(t)
        x = x - t
    return out


def _lane_select(lane, values):
    row = jnp.zeros(lane.shape, F32)
    for i, val in enumerate(values):
        row = jnp.where(lane == i, val, row)
    return row


def _attn_kernel(slopes_ref, q_ref, k_ref, v_ref, lq1_ref, lk1_ref, lq2_ref, lk2_ref, subg_ref, o_ref,
                 kx_ref, qx_ref, s0_ref, s1_ref, p0_ref, p1_ref, a0_ref, a1_ref, m_ref, l_ref, acc_ref, mask_ref,
                 *, lambda_init, seq):
    hd = pl.program_id(1)
    qi = pl.program_id(2)
    slope = slopes_ref[hd] * LOG2_E
    lane = lax.broadcasted_iota(jnp.int32, (8, HEAD_W), 1)

    @pl.when(qi == 0)
    def _():
        rel = (lax.broadcasted_iota(jnp.int32, (ATTN_TQ, ATTN_TK), 0)
               - lax.broadcasted_iota(jnp.int32, (ATTN_TQ, ATTN_TK), 1))
        mask_ref[...] = jnp.where(rel >= 0, 0.0, MASK_VALUE)
        kx_ref[:, :HEAD_W] = k_ref[...]

        def positions(c, carry):
            start = pl.multiple_of(c * ATTN_TK, ATTN_TK)
            pos = start + lax.broadcasted_iota(jnp.int32, (ATTN_TK, HEAD_W), 0)
            ln = lax.broadcasted_iota(jnp.int32, (ATTN_TK, HEAD_W), 1)
            hi = (pos >> 7).astype(F32)
            lo = (pos & 127).astype(F32)
            cols = jnp.where(ln >= 8, 0.0, jnp.where(ln >= 6, 1.0, jnp.where(ln % 2 == 0, hi, lo)))
            kx_ref[pl.ds(start, ATTN_TK), HEAD_W:] = cols.astype(BF16)
            return carry

        lax.fori_loop(0, seq // ATTN_TK, positions, 0)

    q = q_ref[...]
    ql = lax.broadcasted_iota(jnp.int32, q.shape, 1)
    qx_ref[:ATTN_TQ, :HEAD_W] = jnp.where(ql < DA_HEAD_DIM, q, jnp.zeros_like(q))
    qx_ref[ATTN_TQ:, :HEAD_W] = jnp.where(ql >= DA_HEAD_DIM, q, jnp.zeros_like(q))
    svec = jnp.full((8, HEAD_W), slope, F32)
    parts = _split_bf16(svec, 3)
    base = _split_bf16(-svec * (qi * ATTN_TQ).astype(F32), 2)
    coef = _lane_select(lane, [parts[0] * 128.0, parts[0], parts[1] * 128.0, parts[1],
                               parts[2] * 128.0, parts[2], base[0], base[1]])
    qx_ref[:, HEAD_W:] = jnp.broadcast_to(coef[:1], (2 * ATTN_TQ, HEAD_W)).astype(BF16)
    m_ref[...] = jnp.full(m_ref.shape, MASK_VALUE, F32)
    l_ref[...] = jnp.zeros(l_ref.shape, F32)
    acc_ref[...] = jnp.zeros(acc_ref.shape, F32)

    def scores(kj, s_ref):
        start = pl.multiple_of(kj * ATTN_TK, ATTN_TK)
        k = kx_ref[pl.ds(start, ATTN_TK), :]
        s_ref[...] = lax.dot_general(qx_ref[...], k, (((1,), (1,)), ((), ())), preferred_element_type=F32)

    def accumulate(kj, p_ref, alpha_ref):
        start = pl.multiple_of(kj * ATTN_TK, ATTN_TK)
        v = v_ref[pl.ds(start, ATTN_TK), :]
        acc_ref[...] = alpha_ref[...] * acc_ref[...] + jnp.dot(p_ref[...], v, preferred_element_type=F32)

    def softmax(s_ref, p_ref, alpha_ref, masked):
        def block_scores():
            if masked:
                return s_ref[...] + jnp.concatenate([mask_ref[...], mask_ref[...]], axis=0)
            return s_ref[...]
        m_old = m_ref[...]
        m_new = jnp.maximum(m_old, jnp.max(block_scores(), axis=-1, keepdims=True))
        alpha = jnp.exp2(m_old - m_new)
        alpha_ref[...] = alpha
        m_ref[...] = m_new
        s = block_scores()
        ps = [jnp.exp2(s[:, j * HEAD_W:(j + 1) * HEAD_W] - m_new) for j in range(ATTN_TK // HEAD_W)]
        l_ref[...] = alpha * l_ref[...] + functools.reduce(lambda a, b: a + b, ps)
        p_ref[...] = jnp.concatenate(ps, axis=1).astype(BF16)

    s_bufs, p_bufs, a_bufs = (s0_ref, s1_ref), (p0_ref, p1_ref), (a0_ref, a1_ref)
    p1_ref[...] = jnp.zeros(p1_ref.shape, BF16)
    a1_ref[...] = jnp.ones(a1_ref.shape, F32)

    def step(j, par, last=False):
        if not last:
            scores(j + 1, s_bufs[1 - par])
        accumulate(jnp.maximum(j - 1, 0), p_bufs[1 - par], a_bufs[1 - par])
        softmax(s_bufs[par], p_bufs[par], a_bufs[par], masked=last)

    scores(0, s0_ref)

    def pair(t, carry):
        step(2 * t, 0)
        step(2 * t + 1, 1)
        return carry

    lax.fori_loop(0, qi // 2, pair, 0)

    @pl.when(qi % 2 == 0)
    def _():
        step(qi, 0, last=True)
        accumulate(qi, p0_ref, a0_ref)

    @pl.when(qi % 2 == 1)
    def _():
        step(qi - 1, 0)
        step(qi, 1, last=True)
        accumulate(qi, p1_ref, a1_ref)

    lam = (jnp.exp(jnp.sum(lq1_ref[...] * lk1_ref[...], axis=-1, keepdims=True))
           - jnp.exp(jnp.sum(lq2_ref[...] * lk2_ref[...], axis=-1, keepdims=True))
           + lambda_init)
    l = jnp.sum(l_ref[...], axis=-1, keepdims=True)
    o = acc_ref[...] / l
    o = o[:ATTN_TQ] - lam * o[ATTN_TQ:]
    o_ref[...] = (_rms(o, subg_ref[...]) * (1.0 - lambda_init)).astype(BF16)


def _attention(q, kv, slopes, lam_q1, lam_k1, lam_q2, lam_k2, subln_g, batch, seq, lambda_init):
    nq = seq // ATTN_TQ
    vec = lambda a: a.reshape(1, -1).astype(F32)
    lam_spec = _resident((1, DA_HEAD_DIM))
    return pl.pallas_call(
        functools.partial(_attn_kernel, lambda_init=lambda_init, seq=seq),
        grid=(batch, DA_HEADS, nq),
        in_specs=[
            pl.BlockSpec(memory_space=pltpu.SMEM),
            pl.BlockSpec((None, ATTN_TQ, HEAD_W), lambda b, h, i: (b, i, h)),
            pl.BlockSpec((None, seq, HEAD_W), lambda b, h, i: (b, 0, h)),
            pl.BlockSpec((None, seq, HEAD_W), lambda b, h, i: (b, 0, DA_HEADS + h)),
            lam_spec, lam_spec, lam_spec, lam_spec,
            _resident((1, HEAD_W)),
        ],
        out_specs=pl.BlockSpec((None, ATTN_TQ, HEAD_W), lambda b, h, i: (b, i, h)),
        out_shape=jax.ShapeDtypeStruct((batch, seq, DA_HEADS * HEAD_W), BF16),
        scratch_shapes=[
            pltpu.VMEM((seq, 2 * HEAD_W), BF16),
            pltpu.VMEM((2 * ATTN_TQ, 2 * HEAD_W), BF16),
            pltpu.VMEM((2 * ATTN_TQ, ATTN_TK), F32),
            pltpu.VMEM((2 * ATTN_TQ, ATTN_TK), F32),
            pltpu.VMEM((2 * ATTN_TQ, ATTN_TK), BF16),
            pltpu.VMEM((2 * ATTN_TQ, ATTN_TK), BF16),
            pltpu.VMEM((2 * ATTN_TQ, HEAD_W), F32),
            pltpu.VMEM((2 * ATTN_TQ, HEAD_W), F32),
            pltpu.VMEM((2 * ATTN_TQ, HEAD_W), F32),
            pltpu.VMEM((2 * ATTN_TQ, HEAD_W), F32),
            pltpu.VMEM((2 * ATTN_TQ, HEAD_W), F32),
            pltpu.VMEM((ATTN_TQ, ATTN_TK), F32),
        ],
        compiler_params=_params(3),
        name="diff_attn",
    )(slopes, q, kv, kv, vec(lam_q1), vec(lam_k1), vec(lam_q2), vec(lam_k2), vec(subln_g))


def _outproj_kernel(x_ref, a_ref, w_ref, o_ref):
    o_ref[...] = x_ref[...] + jnp.dot(a_ref[...], w_ref[...], preferred_element_type=F32)


def _outproj(x, a, w):
    t, d = x.shape
    row = pl.BlockSpec((FFN_ROWS, d), lambda i: (i, 0))
    return pl.pallas_call(
        _outproj_kernel,
        grid=(t // FFN_ROWS,),
        in_specs=[row, pl.BlockSpec((FFN_ROWS, a.shape[1]), lambda i: (i, 0)), _resident(w.shape)],
        out_specs=row,
        out_shape=jax.ShapeDtypeStruct((t, d), F32),
        compiler_params=_params(1),
        name="outproj",
    )(x, a, w)


def kernel(x, ffn_norm1_g, ffn1_w_gate, ffn1_w_up, ffn1_w_down, mix_norm_g, ffn_norm2_g, ffn2_w_gate, ffn2_w_up, ffn2_w_down, gm_w_in, gm_ln_g, gm_ln_b, gm_w_s, gm_b_s, gm_w_out, kv_norm_g, w_k, w_v, da_w_q, da_lam_q1, da_lam_k1, da_lam_q2, da_lam_k2, da_subln_g, da_w_o, final_norm_g):
    batch, seq, d = x.shape
    depth = ffn_norm1_g.shape[0]
    assert (d, depth, N_A_LAYERS) == (D_MODEL, 2, 1)
    xs = x.reshape(batch * seq, d)
    kv = None
    for l in range(depth):
        if l == N_A_LAYERS:
            w_kv = jnp.concatenate([w_k, w_v], axis=1).astype(BF16)
            kv = _proj(xs, kv_norm_g, w_kv)
        xs = _ffn(xs, ffn_norm1_g[l], *_ffn_weights(ffn1_w_gate[l], ffn1_w_up[l], ffn1_w_down[l]))
        if l < N_A_LAYERS:
            xs = _gmlp(xs, mix_norm_g[l], gm_w_in[l], gm_ln_g[l], gm_ln_b[l], gm_w_s[l], gm_b_s[l], gm_w_out[l])
        else:
            j = l - N_A_LAYERS
            lambda_init = 0.8 - 0.6 * math.exp(-0.3 * l)
            q = _proj(xs, mix_norm_g[l], da_w_q[j].astype(BF16), scale=(DA_HEAD_DIM ** -0.5) * LOG2_E)
            slopes = jnp.exp2(-8.0 * (jnp.arange(DA_HEADS, dtype=F32) + 1.0) / DA_HEADS)
            a = _attention(q.reshape(batch, seq, -1), kv.reshape(batch, seq, -1), slopes,
                           da_lam_q1[j], da_lam_k1[j], da_lam_q2[j], da_lam_k2[j], da_subln_g[j],
                           batch, seq, lambda_init)
            xs = _outproj(xs, a.reshape(batch * seq, -1), da_w_o[j].astype(BF16))
        final_g = final_norm_g if l == depth - 1 else None
        xs = _ffn(xs, ffn_norm2_g[l], *_ffn_weights(ffn2_w_gate[l], ffn2_w_up[l], ffn2_w_down[l]), final_g=final_g)
    return xs.reshape(batch, seq, d)
```

```python
import functools
import math

import jax
import jax.numpy as jnp
from jax import lax
from jax.experimental import pallas as pl
from jax.experimental.pallas import tpu as pltpu

D_MODEL = 1024
D_FF = 2816
GM_HALF = 1024
GM_GROUPS = 8
CHUNK = 128
DA_HEADS = 8
DA_HEAD_DIM = 64
HEAD_W = 2 * DA_HEAD_DIM
RMS_EPS = 1e-6
LN_EPS = 1e-5
N_A_LAYERS = 1

VMEM_LIMIT_BYTES = 56 * 1024 * 1024

FFN_ROWS = 512
FFN_CHUNK = 256
ATTN_TQ = 512
ATTN_TK = 512
MASK_VALUE = -1e30
LOG2_E = 1.4426950408889634
SKIP_LOG2 = 152.0

BF16 = jnp.bfloat16
F32 = jnp.float32


def _params(n_grid):
    return pltpu.CompilerParams(
        dimension_semantics=("arbitrary",) * n_grid,
        vmem_limit_bytes=VMEM_LIMIT_BYTES)


def _resident(shape):
    nd = len(shape)
    return pl.BlockSpec(shape, lambda *_: (0,) * nd, pipeline_mode=pl.Buffered(1))


def _rms(x, g):
    return x * lax.rsqrt(jnp.mean(x * x, axis=-1, keepdims=True) + RMS_EPS) * g


def _ffn_kernel(x_ref, g_ref, wg_ref, wu_ref, wd_ref, *rest, n_chunks, final_norm):
    if final_norm:
        fg_ref, o_ref, h_ref, acc_ref = rest
    else:
        o_ref, h_ref, acc_ref = rest
    h_ref[...] = _rms(x_ref[...], g_ref[...]).astype(BF16)
    for c in range(n_chunks):
        h = h_ref[...]
        gate = jnp.dot(h, wg_ref[c], preferred_element_type=F32)
        up = jnp.dot(h, wu_ref[c], preferred_element_type=F32)
        a = (gate * jax.nn.sigmoid(gate) * up).astype(BF16)
        part = jnp.dot(a, wd_ref[c], preferred_element_type=F32)
        if c == 0:
            acc_ref[...] = part
        else:
            acc_ref[...] += part
    y = x_ref[...] + 0.5 * acc_ref[...]
    if final_norm:
        y = _rms(y, fg_ref[...])
    o_ref[...] = y


def _ffn(x, g, wg, wu, wd, final_g=None):
    t, d = x.shape
    n_chunks = wg.shape[0]
    row = pl.BlockSpec((FFN_ROWS, d), lambda i: (i, 0))
    in_specs = [row, _resident((1, d)), _resident(wg.shape), _resident(wu.shape), _resident(wd.shape)]
    args = [x, g.reshape(1, d), wg, wu, wd]
    if final_g is not None:
        in_specs.append(_resident((1, d)))
        args.append(final_g.reshape(1, d))
    return pl.pallas_call(
        functools.partial(_ffn_kernel, n_chunks=n_chunks, final_norm=final_g is not None),
        grid=(t // FFN_ROWS,),
        in_specs=in_specs,
        out_specs=row,
        out_shape=jax.ShapeDtypeStruct((t, d), F32),
        scratch_shapes=[pltpu.VMEM((FFN_ROWS, d), BF16), pltpu.VMEM((FFN_ROWS, d), F32)],
        compiler_params=_params(1),
        name="ffn_final" if final_g is not None else "ffn",
    )(*args)


def _ffn_weights(w_gate, w_up, w_down):
    d, f = w_gate.shape
    n = f // FFN_CHUNK
    wg = w_gate.astype(BF16).reshape(d, n, FFN_CHUNK).transpose(1, 0, 2)
    wu = w_up.astype(BF16).reshape(d, n, FFN_CHUNK).transpose(1, 0, 2)
    wd = w_down.astype(BF16).reshape(n, FFN_CHUNK, d)
    return wg, wu, wd


def _gelu(z):
    return 0.5 * z * (1.0 + lax.erf(z * (1.0 / math.sqrt(2.0))))


def _gmlp_kernel(x_ref, g_ref, win_ref, lng_ref, lnb_ref, ws_ref, bs_ref, wout_ref, o_ref,
                 h_ref, v_ref, y_ref, *, rows):
    h_ref[...] = _rms(x_ref[...], g_ref[...]).astype(BF16)
    v = _gelu(jnp.dot(h_ref[...], win_ref[:, GM_HALF:], preferred_element_type=F32))
    mu = jnp.mean(v, axis=-1, keepdims=True)
    vc = v - mu
    var = jnp.mean(vc * vc, axis=-1, keepdims=True)
    v_ref[...] = (vc * lax.rsqrt(var + LN_EPS) * lng_ref[...] + lnb_ref[...]).astype(BF16)
    u = _gelu(jnp.dot(h_ref[...], win_ref[:, :GM_HALF], preferred_element_type=F32))
    for n in range(rows // CHUNK):
        r = slice(n * CHUNK, (n + 1) * CHUNK)
        for grp in range(GM_GROUPS):
            c = slice(grp * CHUNK, (grp + 1) * CHUNK)
            sv = jnp.dot(ws_ref[grp], v_ref[r, c], preferred_element_type=F32) + bs_ref[grp]
            y_ref[r, c] = (u[r, c] * sv).astype(BF16)
    o_ref[...] = x_ref[...] + jnp.dot(y_ref[...], wout_ref[...], preferred_element_type=F32)


def _gmlp(x, g, w_in, ln_g, ln_b, w_s, b_s, w_out, rows=256):
    t, d = x.shape
    row = pl.BlockSpec((rows, d), lambda i: (i, 0))
    causal = jnp.tril(jnp.ones((CHUNK, CHUNK), dtype=bool))
    ws = jnp.where(causal[None], w_s, jnp.zeros_like(w_s)).astype(BF16)
    bs = jnp.broadcast_to(b_s[:, :, None], (GM_GROUPS, CHUNK, CHUNK)).astype(F32)
    args = [x, g.reshape(1, d), w_in.astype(BF16), ln_g.reshape(1, GM_HALF), ln_b.reshape(1, GM_HALF),
            ws, bs, w_out.astype(BF16)]
    in_specs = [row] + [_resident(a.shape) for a in args[1:]]
    return pl.pallas_call(
        functools.partial(_gmlp_kernel, rows=rows),
        grid=(t // rows,),
        in_specs=in_specs,
        out_specs=row,
        out_shape=jax.ShapeDtypeStruct((t, d), F32),
        scratch_shapes=[pltpu.VMEM((rows, d), BF16), pltpu.VMEM((rows, GM_HALF), BF16),
                        pltpu.VMEM((rows, GM_HALF), BF16)],
        compiler_params=_params(1),
        name="gmlp",
    )(*args)


def _proj_kernel(x_ref, g_ref, w_ref, o_ref, *, scale):
    h = _rms(x_ref[...], g_ref[...]).astype(BF16)
    o_ref[...] = (jnp.dot(h, w_ref[...], preferred_element_type=F32) * scale).astype(BF16)


def _proj(x, g, w, scale=1.0):
    t, d = x.shape
    n = w.shape[1]
    return pl.pallas_call(
        functools.partial(_proj_kernel, scale=scale),
        grid=(t // FFN_ROWS,),
        in_specs=[pl.BlockSpec((FFN_ROWS, d), lambda i: (i, 0)), _resident((1, d)), _resident(w.shape)],
        out_specs=pl.BlockSpec((FFN_ROWS, n), lambda i: (i, 0)),
        out_shape=jax.ShapeDtypeStruct((t, n), BF16),
        compiler_params=_params(1),
        name="proj",
    )(x, g.reshape(1, d), w)


def _split_bf16(x, parts):
    out = []
    for _ in range(parts):
        t = x.astype(BF16).astype(F32)
        out.append(t)
        x = x - t
    return out


def _lane_select(lane, values):
    row = jnp.zeros(lane.shape, F32)
    for i, val in enumerate(values):
        row = jnp.where(lane == i, val, row)
    return row


def _half_norms_sq(x):
    ln = lax.broadcasted_iota(jnp.int32, x.shape, 1)
    x2 = x * x
    left = jnp.sum(jnp.where(ln < DA_HEAD_DIM, x2, 0.0), axis=-1, keepdims=True)
    right = jnp.sum(jnp.where(ln >= DA_HEAD_DIM, x2, 0.0), axis=-1, keepdims=True)
    return jnp.maximum(left, right)


def _attn_kernel(slopes_ref, q_ref, k_ref, v_ref, lq1_ref, lk1_ref, lq2_ref, lk2_ref, subg_ref, o_ref,
                 kx_ref, qx_ref, s0_ref, s1_ref, p0_ref, p1_ref, a0_ref, a1_ref, m_ref, l_ref, acc_ref, mask_ref,
                 knorm_ref, *, lambda_init, seq):
    hd = pl.program_id(1)
    qi = pl.program_id(2)
    slope = slopes_ref[hd] * LOG2_E
    lane = lax.broadcasted_iota(jnp.int32, (8, HEAD_W), 1)

    @pl.when(qi == 0)
    def _():
        rel = (lax.broadcasted_iota(jnp.int32, (ATTN_TQ, ATTN_TK), 0)
               - lax.broadcasted_iota(jnp.int32, (ATTN_TQ, ATTN_TK), 1))
        mask_ref[...] = jnp.where(rel >= 0, 0.0, MASK_VALUE)
        kx_ref[:, :HEAD_W] = k_ref[...]

        def positions(c, carry):
            start = pl.multiple_of(c * ATTN_TK, ATTN_TK)
            pos = start + lax.broadcasted_iota(jnp.int32, (ATTN_TK, HEAD_W), 0)
            ln = lax.broadcasted_iota(jnp.int32, (ATTN_TK, HEAD_W), 1)
            hi = (pos >> 7).astype(F32)
            lo = (pos & 127).astype(F32)
            cols = jnp.where(ln >= 8, 0.0, jnp.where(ln >= 6, 1.0, jnp.where(ln % 2 == 0, hi, lo)))
            kx_ref[pl.ds(start, ATTN_TK), HEAD_W:] = cols.astype(BF16)
            return carry

        lax.fori_loop(0, seq // ATTN_TK, positions, 0)

        def key_norms(c, best):
            start = pl.multiple_of(c * ATTN_TK, ATTN_TK)
            k = k_ref[pl.ds(start, ATTN_TK), :].astype(F32)
            return jnp.maximum(best, _half_norms_sq(k))

        best = lax.fori_loop(0, seq // ATTN_TK, key_norms, jnp.zeros((ATTN_TK, 1), F32))
        knorm_ref[...] = jnp.broadcast_to(jnp.max(best, axis=0, keepdims=True), knorm_ref.shape)

    q = q_ref[...]
    ql = lax.broadcasted_iota(jnp.int32, q.shape, 1)
    qx_ref[:ATTN_TQ, :HEAD_W] = jnp.where(ql < DA_HEAD_DIM, q, jnp.zeros_like(q))
    qx_ref[ATTN_TQ:, :HEAD_W] = jnp.where(ql >= DA_HEAD_DIM, q, jnp.zeros_like(q))
    svec = jnp.full((8, HEAD_W), slope, F32)
    parts = _split_bf16(svec, 3)
    base = _split_bf16(-svec * (qi * ATTN_TQ).astype(F32), 2)
    coef = _lane_select(lane, [parts[0] * 128.0, parts[0], parts[1] * 128.0, parts[1],
                               parts[2] * 128.0, parts[2], base[0], base[1]])
    qx_ref[:, HEAD_W:] = jnp.broadcast_to(coef[:1], (2 * ATTN_TQ, HEAD_W)).astype(BF16)
    m_ref[...] = jnp.full(m_ref.shape, MASK_VALUE, F32)
    l_ref[...] = jnp.zeros(l_ref.shape, F32)
    acc_ref[...] = jnp.zeros(acc_ref.shape, F32)

    def scores(kj, s_ref):
        start = pl.multiple_of(kj * ATTN_TK, ATTN_TK)
        k = kx_ref[pl.ds(start, ATTN_TK), :]
        s_ref[...] = lax.dot_general(qx_ref[...], k, (((1,), (1,)), ((), ())), preferred_element_type=F32)

    def accumulate(kj, p_ref, alpha_ref):
        start = pl.multiple_of(kj * ATTN_TK, ATTN_TK)
        v = v_ref[pl.ds(start, ATTN_TK), :]
        acc_ref[...] = alpha_ref[...] * acc_ref[...] + jnp.dot(p_ref[...], v, preferred_element_type=F32)

    def softmax(s_ref, p_ref, alpha_ref, masked):
        def block_scores():
            if masked:
                return s_ref[...] + jnp.concatenate([mask_ref[...], mask_ref[...]], axis=0)
            return s_ref[...]
        m_old = m_ref[...]
        m_new = jnp.maximum(m_old, jnp.max(block_scores(), axis=-1, keepdims=True))
        alpha = jnp.exp2(m_old - m_new)
        alpha_ref[...] = alpha
        m_ref[...] = m_new
        s = block_scores()
        ps = [jnp.exp2(s[:, j * HEAD_W:(j + 1) * HEAD_W] - m_new) for j in range(ATTN_TK // HEAD_W)]
        l_ref[...] = alpha * l_ref[...] + functools.reduce(lambda a, b: a + b, ps)
        p_ref[...] = jnp.concatenate(ps, axis=1).astype(BF16)

    qnorm = jnp.max(_half_norms_sq(q.astype(F32)), axis=0, keepdims=True)
    bound = jnp.sqrt(qnorm * knorm_ref[:1, :1]) * 1.01 + 1.0
    reach = (2.0 * bound + SKIP_LOG2) / slope
    first = jnp.floor(((qi * ATTN_TQ).astype(F32) - reach) * (1.0 / ATTN_TK))
    n_far = qi - jnp.clip(first, 0.0, qi.astype(F32)).astype(jnp.int32)[0, 0]

    s_bufs, p_bufs, a_bufs = (s0_ref, s1_ref), (p0_ref, p1_ref), (a0_ref, a1_ref)

    def step(t, par, last=False):
        accumulate(qi - (t - 1), p_bufs[1 - par], a_bufs[1 - par])
        softmax(s_bufs[par], p_bufs[par], a_bufs[par], masked=False)
        if not last:
            scores(qi - jnp.minimum(t + 1, n_far), s_bufs[1 - par])

    scores(qi, s0_ref)
    softmax(s0_ref, p0_ref, a0_ref, masked=True)
    scores(qi - jnp.minimum(1, n_far), s1_ref)

    def pair(u, carry):
        step(2 * u + 1, 1)
        step(2 * u + 2, 0)
        return carry

    lax.fori_loop(0, n_far // 2, pair, 0)

    @pl.when(n_far % 2 == 1)
    def _():
        step(n_far, 1, last=True)
        accumulate(qi - n_far, p1_ref, a1_ref)

    @pl.when(n_far % 2 == 0)
    def _():
        accumulate(qi - n_far, p0_ref, a0_ref)

    lam = (jnp.exp(jnp.sum(lq1_ref[...] * lk1_ref[...], axis=-1, keepdims=True))
           - jnp.exp(jnp.sum(lq2_ref[...] * lk2_ref[...], axis=-1, keepdims=True))
           + lambda_init)
    l = jnp.sum(l_ref[...], axis=-1, keepdims=True)
    o = acc_ref[...] / l
    o = o[:ATTN_TQ] - lam * o[ATTN_TQ:]
    o_ref[...] = (_rms(o, subg_ref[...]) * (1.0 - lambda_init)).astype(BF16)


def _attention(q, kv, slopes, lam_q1, lam_k1, lam_q2, lam_k2, subln_g, batch, seq, lambda_init):
    nq = seq // ATTN_TQ
    vec = lambda a: a.reshape(1, -1).astype(F32)
    lam_spec = _resident((1, DA_HEAD_DIM))
    return pl.pallas_call(
        functools.partial(_attn_kernel, lambda_init=lambda_init, seq=seq),
        grid=(batch, DA_HEADS, nq),
        in_specs=[
            pl.BlockSpec(memory_space=pltpu.SMEM),
            pl.BlockSpec((None, ATTN_TQ, HEAD_W), lambda b, h, i: (b, i, h)),
            pl.BlockSpec((None, seq, HEAD_W), lambda b, h, i: (b, 0, h)),
            pl.BlockSpec((None, seq, HEAD_W), lambda b, h, i: (b, 0, DA_HEADS + h)),
            lam_spec, lam_spec, lam_spec, lam_spec,
            _resident((1, HEAD_W)),
        ],
        out_specs=pl.BlockSpec((None, ATTN_TQ, HEAD_W), lambda b, h, i: (b, i, h)),
        out_shape=jax.ShapeDtypeStruct((batch, seq, DA_HEADS * HEAD_W), BF16),
        scratch_shapes=[
            pltpu.VMEM((seq, 2 * HEAD_W), BF16),
            pltpu.VMEM((2 * ATTN_TQ, 2 * HEAD_W), BF16),
            pltpu.VMEM((2 * ATTN_TQ, ATTN_TK), F32),
            pltpu.VMEM((2 * ATTN_TQ, ATTN_TK), F32),
            pltpu.VMEM((2 * ATTN_TQ, ATTN_TK), BF16),
            pltpu.VMEM((2 * ATTN_TQ, ATTN_TK), BF16),
            pltpu.VMEM((2 * ATTN_TQ, HEAD_W), F32),
            pltpu.VMEM((2 * ATTN_TQ, HEAD_W), F32),
            pltpu.VMEM((2 * ATTN_TQ, HEAD_W), F32),
            pltpu.VMEM((2 * ATTN_TQ, HEAD_W), F32),
            pltpu.VMEM((2 * ATTN_TQ, HEAD_W), F32),
            pltpu.VMEM((ATTN_TQ, ATTN_TK), F32),
            pltpu.VMEM((8, HEAD_W), F32),
        ],
        compiler_params=_params(3),
        name="diff_attn",
    )(slopes, q, kv, kv, vec(lam_q1), vec(lam_k1), vec(lam_q2), vec(lam_k2), vec(subln_g))


def _outproj_kernel(x_ref, a_ref, w_ref, o_ref):
    o_ref[...] = x_ref[...] + jnp.dot(a_ref[...], w_ref[...], preferred_element_type=F32)


def _outproj(x, a, w):
    t, d = x.shape
    row = pl.BlockSpec((FFN_ROWS, d), lambda i: (i, 0))
    return pl.pallas_call(
        _outproj_kernel,
        grid=(t // FFN_ROWS,),
        in_specs=[row, pl.BlockSpec((FFN_ROWS, a.shape[1]), lambda i: (i, 0)), _resident(w.shape)],
        out_specs=row,
        out_shape=jax.ShapeDtypeStruct((t, d), F32),
        compiler_params=_params(1),
        name="outproj",
    )(x, a, w)


def kernel(x, ffn_norm1_g, ffn1_w_gate, ffn1_w_up, ffn1_w_down, mix_norm_g, ffn_norm2_g, ffn2_w_gate, ffn2_w_up, ffn2_w_down, gm_w_in, gm_ln_g, gm_ln_b, gm_w_s, gm_b_s, gm_w_out, kv_norm_g, w_k, w_v, da_w_q, da_lam_q1, da_lam_k1, da_lam_q2, da_lam_k2, da_subln_g, da_w_o, final_norm_g):
    batch, seq, d = x.shape
    depth = ffn_norm1_g.shape[0]
    assert (d, depth, N_A_LAYERS) == (D_MODEL, 2, 1)
    xs = x.reshape(batch * seq, d)
    kv = None
    for l in range(depth):
        if l == N_A_LAYERS:
            w_kv = jnp.concatenate([w_k, w_v], axis=1).astype(BF16)
            kv = _proj(xs, kv_norm_g, w_kv)
        xs = _ffn(xs, ffn_norm1_g[l], *_ffn_weights(ffn1_w_gate[l], ffn1_w_up[l], ffn1_w_down[l]))
        if l < N_A_LAYERS:
            xs = _gmlp(xs, mix_norm_g[l], gm_w_in[l], gm_ln_g[l], gm_ln_b[l], gm_w_s[l], gm_b_s[l], gm_w_out[l])
        else:
            j = l - N_A_LAYERS
            lambda_init = 0.8 - 0.6 * math.exp(-0.3 * l)
            q = _proj(xs, mix_norm_g[l], da_w_q[j].astype(BF16), scale=(DA_HEAD_DIM ** -0.5) * LOG2_E)
            slopes = jnp.exp2(-8.0 * (jnp.arange(DA_HEADS, dtype=F32) + 1.0) / DA_HEADS)
            a = _attention(q.reshape(batch, seq, -1), kv.reshape(batch, seq, -1), slopes,
                           da_lam_q1[j], da_lam_k1[j], da_lam_q2[j], da_lam_k2[j], da_subln_g[j],
                           batch, seq, lambda_init)
            xs = _outproj(xs, a.reshape(batch * seq, -1), da_w_o[j].astype(BF16))
        final_g = final_norm_g if l == depth - 1 else None
        xs = _ffn(xs, ffn_norm2_g[l], *_ffn_weights(ffn2_w_gate[l], ffn2_w_up[l], ffn2_w_down[l]), final_g=final_g)
    return xs.reshape(batch, seq, d)
```

```python
import functools
import math

import jax
import jax.numpy as jnp
from jax import lax
from jax.experimental import pallas as pl
from jax.experimental.pallas import tpu as pltpu

D_MODEL = 1024
D_FF = 2816
GM_HALF = 1024
GM_GROUPS = 8
CHUNK = 128
DA_HEADS = 8
DA_HEAD_DIM = 64
HEAD_W = 2 * DA_HEAD_DIM
RMS_EPS = 1e-6
LN_EPS = 1e-5
N_A_LAYERS = 1

VMEM_LIMIT_BYTES = 56 * 1024 * 1024

FFN_ROWS = 512
FFN_CHUNK = 256
ATTN_TQ = 512
ATTN_TK = 512
MASK_VALUE = -1e30
LOG2_E = 1.4426950408889634
SKIP_LOG2 = 152.0

BF16 = jnp.bfloat16
F32 = jnp.float32


def _params(n_grid):
    return pltpu.CompilerParams(
        dimension_semantics=("arbitrary",) * n_grid,
        vmem_limit_bytes=VMEM_LIMIT_BYTES)


def _resident(shape):
    nd = len(shape)
    return pl.BlockSpec(shape, lambda *_: (0,) * nd, pipeline_mode=pl.Buffered(1))


def _rms(x, g):
    return x * lax.rsqrt(jnp.mean(x * x, axis=-1, keepdims=True) + RMS_EPS) * g


def _ffn_kernel(*refs, has_pre, has_proj, proj_scale, final_norm):
    it = iter(refs)
    x_ref = next(it)
    if has_pre:
        a_ref, wpre_ref = next(it), next(it)
    g_ref, wg_ref, wu_ref, wd_ref = next(it), next(it), next(it), next(it)
    if has_proj:
        pg_ref, pw_ref = next(it), next(it)
    if final_norm:
        fg_ref = next(it)
    o_ref = next(it)
    if has_proj:
        po_ref = next(it)
    h_ref, acc_ref = next(it), next(it)

    x = x_ref[...]
    if has_pre:
        x = x + jnp.dot(a_ref[...], wpre_ref[...], preferred_element_type=F32)
    o_ref[...] = x
    h_ref[...] = _rms(x, g_ref[...]).astype(BF16)
    for c in range(D_FF // FFN_CHUNK):
        cols = slice(c * FFN_CHUNK, (c + 1) * FFN_CHUNK)
        h = h_ref[...]
        gate = jnp.dot(h, wg_ref[:, cols], preferred_element_type=F32)
        up = jnp.dot(h, wu_ref[:, cols], preferred_element_type=F32)
        a = (gate * jax.nn.sigmoid(gate) * up).astype(BF16)
        part = jnp.dot(a, wd_ref[cols, :], preferred_element_type=F32)
        if c == 0:
            acc_ref[...] = part
        else:
            acc_ref[...] += part
    y = o_ref[...] + 0.5 * acc_ref[...]
    if has_proj:
        hp = _rms(y, pg_ref[...]).astype(BF16)
        po_ref[...] = (jnp.dot(hp, pw_ref[...], preferred_element_type=F32) * proj_scale).astype(BF16)
    if final_norm:
        y = _rms(y, fg_ref[...])
    o_ref[...] = y


def _ffn(x, g, w_gate, w_up, w_down, pre=None, proj=None, final_g=None):
    t, d = x.shape
    row = pl.BlockSpec((FFN_ROWS, d), lambda i: (i, 0))
    args, in_specs = [x], [row]
    if pre is not None:
        a, w_pre = pre
        args += [a, w_pre.astype(BF16)]
        in_specs += [pl.BlockSpec((FFN_ROWS, a.shape[1]), lambda i: (i, 0)), _resident(w_pre.shape)]
    args += [g.reshape(1, d), w_gate.astype(BF16), w_up.astype(BF16), w_down.astype(BF16)]
    in_specs += [_resident((1, d)), _resident(w_gate.shape), _resident(w_up.shape), _resident(w_down.shape)]
    out_shape, out_specs = [jax.ShapeDtypeStruct((t, d), F32)], [row]
    proj_scale = 1.0
    if proj is not None:
        pg, pw, proj_scale = proj
        args += [pg.reshape(1, d), pw.astype(BF16)]
        in_specs += [_resident((1, d)), _resident(pw.shape)]
        out_shape.append(jax.ShapeDtypeStruct((t, pw.shape[1]), BF16))
        out_specs.append(pl.BlockSpec((FFN_ROWS, pw.shape[1]), lambda i: (i, 0)))
    if final_g is not None:
        args.append(final_g.reshape(1, d))
        in_specs.append(_resident((1, d)))
    outs = pl.pallas_call(
        functools.partial(_ffn_kernel, has_pre=pre is not None, has_proj=proj is not None,
                          proj_scale=proj_scale, final_norm=final_g is not None),
        grid=(t // FFN_ROWS,),
        in_specs=in_specs,
        out_specs=out_specs,
        out_shape=out_shape,
        scratch_shapes=[pltpu.VMEM((FFN_ROWS, d), BF16), pltpu.VMEM((FFN_ROWS, d), F32)],
        compiler_params=_params(1),
        name="ffn",
    )(*args)
    return outs if proj is not None else outs[0]


def _gelu(z):
    return 0.5 * z * (1.0 + lax.erf(z * (1.0 / math.sqrt(2.0))))


def _gmlp_kernel(x_ref, g_ref, win_ref, lng_ref, lnb_ref, ws_ref, bs_ref, wout_ref, o_ref,
                 h_ref, v_ref, y_ref, *, rows):
    h_ref[...] = _rms(x_ref[...], g_ref[...]).astype(BF16)
    v = _gelu(jnp.dot(h_ref[...], win_ref[:, GM_HALF:], preferred_element_type=F32))
    mu = jnp.mean(v, axis=-1, keepdims=True)
    vc = v - mu
    var = jnp.mean(vc * vc, axis=-1, keepdims=True)
    v_ref[...] = (vc * lax.rsqrt(var + LN_EPS) * lng_ref[...] + lnb_ref[...]).astype(BF16)
    u = _gelu(jnp.dot(h_ref[...], win_ref[:, :GM_HALF], preferred_element_type=F32))
    for n in range(rows // CHUNK):
        r = slice(n * CHUNK, (n + 1) * CHUNK)
        for grp in range(GM_GROUPS):
            c = slice(grp * CHUNK, (grp + 1) * CHUNK)
            sv = jnp.dot(ws_ref[grp], v_ref[r, c], preferred_element_type=F32) + bs_ref[grp]
            y_ref[r, c] = (u[r, c] * sv).astype(BF16)
    o_ref[...] = x_ref[...] + jnp.dot(y_ref[...], wout_ref[...], preferred_element_type=F32)


def _gmlp(x, g, w_in, ln_g, ln_b, w_s, b_s, w_out, rows=256):
    t, d = x.shape
    row = pl.BlockSpec((rows, d), lambda i: (i, 0))
    causal = jnp.tril(jnp.ones((CHUNK, CHUNK), dtype=bool))
    ws = jnp.where(causal[None], w_s, jnp.zeros_like(w_s)).astype(BF16)
    bs = jnp.broadcast_to(b_s[:, :, None], (GM_GROUPS, CHUNK, CHUNK)).astype(F32)
    args = [x, g.reshape(1, d), w_in.astype(BF16), ln_g.reshape(1, GM_HALF), ln_b.reshape(1, GM_HALF),
            ws, bs, w_out.astype(BF16)]
    in_specs = [row] + [_resident(a.shape) for a in args[1:]]
    return pl.pallas_call(
        functools.partial(_gmlp_kernel, rows=rows),
        grid=(t // rows,),
        in_specs=in_specs,
        out_specs=row,
        out_shape=jax.ShapeDtypeStruct((t, d), F32),
        scratch_shapes=[pltpu.VMEM((rows, d), BF16), pltpu.VMEM((rows, GM_HALF), BF16),
                        pltpu.VMEM((rows, GM_HALF), BF16)],
        compiler_params=_params(1),
        name="gmlp",
    )(*args)


def _split_bf16(x, parts):
    out = []
    for _ in range(parts):
        t = x.astype(BF16).astype(F32)
        out.append(t)
        x = x - t
    return out


def _lane_select(lane, values):
    row = jnp.zeros(lane.shape, F32)
    for i, val in enumerate(values):
        row = jnp.where(lane == i, val, row)
    return row


def _half_norms_sq(x):
    ln = lax.broadcasted_iota(jnp.int32, x.shape, 1)
    x2 = x * x
    left = jnp.sum(jnp.where(ln < DA_HEAD_DIM, x2, 0.0), axis=-1, keepdims=True)
    right = jnp.sum(jnp.where(ln >= DA_HEAD_DIM, x2, 0.0), axis=-1, keepdims=True)
    return jnp.maximum(left, right)


def _attn_kernel(slopes_ref, q_ref, k_ref, v_ref, lq1_ref, lk1_ref, lq2_ref, lk2_ref, subg_ref, o_ref,
                 kx_ref, qx_ref, s0_ref, s1_ref, p0_ref, p1_ref, a0_ref, a1_ref, m_ref, l_ref, acc_ref, mask_ref,
                 knorm_ref, *, lambda_init, seq):
    hd = pl.program_id(1)
    qi = pl.program_id(2)
    slope = slopes_ref[hd] * LOG2_E
    lane = lax.broadcasted_iota(jnp.int32, (8, HEAD_W), 1)

    @pl.when(qi == 0)
    def _():
        rel = (lax.broadcasted_iota(jnp.int32, (ATTN_TQ, ATTN_TK), 0)
               - lax.broadcasted_iota(jnp.int32, (ATTN_TQ, ATTN_TK), 1))
        mask_ref[...] = jnp.where(rel >= 0, 0.0, MASK_VALUE)
        kx_ref[:, :HEAD_W] = k_ref[...]

        def positions(c, carry):
            start = pl.multiple_of(c * ATTN_TK, ATTN_TK)
            pos = start + lax.broadcasted_iota(jnp.int32, (ATTN_TK, HEAD_W), 0)
            ln = lax.broadcasted_iota(jnp.int32, (ATTN_TK, HEAD_W), 1)
            hi = (pos >> 7).astype(F32)
            lo = (pos & 127).astype(F32)
            cols = jnp.where(ln >= 8, 0.0, jnp.where(ln >= 6, 1.0, jnp.where(ln % 2 == 0, hi, lo)))
            kx_ref[pl.ds(start, ATTN_TK), HEAD_W:] = cols.astype(BF16)
            return carry

        lax.fori_loop(0, seq // ATTN_TK, positions, 0)

        def key_norms(c, best):
            start = pl.multiple_of(c * ATTN_TK, ATTN_TK)
            k = k_ref[pl.ds(start, ATTN_TK), :].astype(F32)
            return jnp.maximum(best, _half_norms_sq(k))

        best = lax.fori_loop(0, seq // ATTN_TK, key_norms, jnp.zeros((ATTN_TK, 1), F32))
        knorm_ref[...] = jnp.broadcast_to(jnp.max(best, axis=0, keepdims=True), knorm_ref.shape)

    q = q_ref[...]
    ql = lax.broadcasted_iota(jnp.int32, q.shape, 1)
    qx_ref[:ATTN_TQ, :HEAD_W] = jnp.where(ql < DA_HEAD_DIM, q, jnp.zeros_like(q))
    qx_ref[ATTN_TQ:, :HEAD_W] = jnp.where(ql >= DA_HEAD_DIM, q, jnp.zeros_like(q))
    svec = jnp.full((8, HEAD_W), slope, F32)
    parts = _split_bf16(svec, 3)
    base = _split_bf16(-svec * (qi * ATTN_TQ).astype(F32), 2)
    coef = _lane_select(lane, [parts[0] * 128.0, parts[0], parts[1] * 128.0, parts[1],
                               parts[2] * 128.0, parts[2], base[0], base[1]])
    qx_ref[:, HEAD_W:] = jnp.broadcast_to(coef[:1], (2 * ATTN_TQ, HEAD_W)).astype(BF16)
    m_ref[...] = jnp.full(m_ref.shape, MASK_VALUE, F32)
    l_ref[...] = jnp.zeros(l_ref.shape, F32)
    acc_ref[...] = jnp.zeros(acc_ref.shape, F32)

    def scores(kj, s_ref):
        start = pl.multiple_of(kj * ATTN_TK, ATTN_TK)
        k = kx_ref[pl.ds(start, ATTN_TK), :]
        s_ref[...] = lax.dot_general(qx_ref[...], k, (((1,), (1,)), ((), ())), preferred_element_type=F32)

    def accumulate(kj, p_ref, alpha_ref):
        start = pl.multiple_of(kj * ATTN_TK, ATTN_TK)
        v = v_ref[pl.ds(start, ATTN_TK), :]
        acc_ref[...] = alpha_ref[...] * acc_ref[...] + jnp.dot(p_ref[...], v, preferred_element_type=F32)

    def softmax(s_ref, p_ref, alpha_ref, masked):
        def block_scores():
            if masked:
                return s_ref[...] + jnp.concatenate([mask_ref[...], mask_ref[...]], axis=0)
            return s_ref[...]
        m_old = m_ref[...]
        m_new = jnp.maximum(m_old, jnp.max(block_scores(), axis=-1, keepdims=True))
        alpha = jnp.exp2(m_old - m_new)
        alpha_ref[...] = alpha
        m_ref[...] = m_new
        s = block_scores()
        ps = [jnp.exp2(s[:, j * HEAD_W:(j + 1) * HEAD_W] - m_new) for j in range(ATTN_TK // HEAD_W)]
        l_ref[...] = alpha * l_ref[...] + functools.reduce(lambda a, b: a + b, ps)
        p_ref[...] = jnp.concatenate(ps, axis=1).astype(BF16)

    qnorm = jnp.max(_half_norms_sq(q.astype(F32)), axis=0, keepdims=True)
    bound = jnp.sqrt(qnorm * knorm_ref[:1, :1]) * 1.01 + 1.0
    reach = (2.0 * bound + SKIP_LOG2) / slope
    first = jnp.floor(((qi * ATTN_TQ).astype(F32) - reach) * (1.0 / ATTN_TK))
    n_far = qi - jnp.clip(first, 0.0, qi.astype(F32)).astype(jnp.int32)[0, 0]

    s_bufs, p_bufs, a_bufs = (s0_ref, s1_ref), (p0_ref, p1_ref), (a0_ref, a1_ref)

    def step(t, par, last=False):
        accumulate(qi - (t - 1), p_bufs[1 - par], a_bufs[1 - par])
        softmax(s_bufs[par], p_bufs[par], a_bufs[par], masked=False)
        if not last:
            scores(qi - jnp.minimum(t + 1, n_far), s_bufs[1 - par])

    scores(qi, s0_ref)
    softmax(s0_ref, p0_ref, a0_ref, masked=True)
    scores(qi - jnp.minimum(1, n_far), s1_ref)

    def pair(u, carry):
        step(2 * u + 1, 1)
        step(2 * u + 2, 0)
        return carry

    lax.fori_loop(0, n_far // 2, pair, 0)

    @pl.when(n_far % 2 == 1)
    def _():
        step(n_far, 1, last=True)
        accumulate(qi - n_far, p1_ref, a1_ref)

    @pl.when(n_far % 2 == 0)
    def _():
        accumulate(qi - n_far, p0_ref, a0_ref)

    lam = (jnp.exp(jnp.sum(lq1_ref[...] * lk1_ref[...], axis=-1, keepdims=True))
           - jnp.exp(jnp.sum(lq2_ref[...] * lk2_ref[...], axis=-1, keepdims=True))
           + lambda_init)
    l = jnp.sum(l_ref[...], axis=-1, keepdims=True)
    o = acc_ref[...] / l
    o = o[:ATTN_TQ] - lam * o[ATTN_TQ:]
    o_ref[...] = (_rms(o, subg_ref[...]) * (1.0 - lambda_init)).astype(BF16)


def _attention(q, kv, slopes, lam_q1, lam_k1, lam_q2, lam_k2, subln_g, batch, seq, lambda_init):
    nq = seq // ATTN_TQ
    vec = lambda a: a.reshape(1, -1).astype(F32)
    lam_spec = _resident((1, DA_HEAD_DIM))
    return pl.pallas_call(
        functools.partial(_attn_kernel, lambda_init=lambda_init, seq=seq),
        grid=(batch, DA_HEADS, nq),
        in_specs=[
            pl.BlockSpec(memory_space=pltpu.SMEM),
            pl.BlockSpec((None, ATTN_TQ, HEAD_W), lambda b, h, i: (b, i, h)),
            pl.BlockSpec((None, seq, HEAD_W), lambda b, h, i: (b, 0, h)),
            pl.BlockSpec((None, seq, HEAD_W), lambda b, h, i: (b, 0, DA_HEADS + h)),
            lam_spec, lam_spec, lam_spec, lam_spec,
            _resident((1, HEAD_W)),
        ],
        out_specs=pl.BlockSpec((None, ATTN_TQ, HEAD_W), lambda b, h, i: (b, i, h)),
        out_shape=jax.ShapeDtypeStruct((batch, seq, DA_HEADS * HEAD_W), BF16),
        scratch_shapes=[
            pltpu.VMEM((seq, 2 * HEAD_W), BF16),
            pltpu.VMEM((2 * ATTN_TQ, 2 * HEAD_W), BF16),
            pltpu.VMEM((2 * ATTN_TQ, ATTN_TK), F32),
            pltpu.VMEM((2 * ATTN_TQ, ATTN_TK), F32),
            pltpu.VMEM((2 * ATTN_TQ, ATTN_TK), BF16),
            pltpu.VMEM((2 * ATTN_TQ, ATTN_TK), BF16),
            pltpu.VMEM((2 * ATTN_TQ, HEAD_W), F32),
            pltpu.VMEM((2 * ATTN_TQ, HEAD_W), F32),
            pltpu.VMEM((2 * ATTN_TQ, HEAD_W), F32),
            pltpu.VMEM((2 * ATTN_TQ, HEAD_W), F32),
            pltpu.VMEM((2 * ATTN_TQ, HEAD_W), F32),
            pltpu.VMEM((ATTN_TQ, ATTN_TK), F32),
            pltpu.VMEM((8, HEAD_W), F32),
        ],
        compiler_params=_params(3),
        name="diff_attn",
    )(slopes, q, kv, kv, vec(lam_q1), vec(lam_k1), vec(lam_q2), vec(lam_k2), vec(subln_g))


def kernel(x, ffn_norm1_g, ffn1_w_gate, ffn1_w_up, ffn1_w_down, mix_norm_g, ffn_norm2_g, ffn2_w_gate, ffn2_w_up, ffn2_w_down, gm_w_in, gm_ln_g, gm_ln_b, gm_w_s, gm_b_s, gm_w_out, kv_norm_g, w_k, w_v, da_w_q, da_lam_q1, da_lam_k1, da_lam_q2, da_lam_k2, da_subln_g, da_w_o, final_norm_g):
    batch, seq, d = x.shape
    depth = ffn_norm1_g.shape[0]
    assert (d, depth, N_A_LAYERS) == (D_MODEL, 2, 1)
    xs = x.reshape(batch * seq, d)

    def ffn1(l, x_in, **fused):
        return _ffn(x_in, ffn_norm1_g[l], ffn1_w_gate[l], ffn1_w_up[l], ffn1_w_down[l], **fused)

    def ffn2(l, x_in, **fused):
        return _ffn(x_in, ffn_norm2_g[l], ffn2_w_gate[l], ffn2_w_up[l], ffn2_w_down[l], **fused)

    xs = ffn1(0, xs)
    xs = _gmlp(xs, mix_norm_g[0], gm_w_in[0], gm_ln_g[0], gm_ln_b[0], gm_w_s[0], gm_b_s[0], gm_w_out[0])
    xs, kv = ffn2(0, xs, proj=(kv_norm_g, jnp.concatenate([w_k, w_v], axis=1), 1.0))

    lambda_init = 0.8 - 0.6 * math.exp(-0.3 * 1)
    xs, q = ffn1(1, xs, proj=(mix_norm_g[1], da_w_q[0], (DA_HEAD_DIM ** -0.5) * LOG2_E))
    slopes = jnp.exp2(-8.0 * (jnp.arange(DA_HEADS, dtype=F32) + 1.0) / DA_HEADS)
    a = _attention(q.reshape(batch, seq, -1), kv.reshape(batch, seq, -1), slopes,
                   da_lam_q1[0], da_lam_k1[0], da_lam_q2[0], da_lam_k2[0], da_subln_g[0],
                   batch, seq, lambda_init)
    xs = ffn2(1, xs, pre=(a.reshape(batch * seq, -1), da_w_o[0]), final_g=final_norm_g)
    return xs.reshape(batch, seq, d)
```

```python
import functools
import math

import jax
import jax.numpy as jnp
from jax import lax
from jax.experimental import pallas as pl
from jax.experimental.pallas import tpu as pltpu

D_MODEL = 1024
D_FF = 2816
GM_HALF = 1024
GM_GROUPS = 8
CHUNK = 128
DA_HEADS = 8
DA_HEAD_DIM = 64
HEAD_W = 2 * DA_HEAD_DIM
RMS_EPS = 1e-6
LN_EPS = 1e-5
N_A_LAYERS = 1

VMEM_LIMIT_BYTES = 56 * 1024 * 1024

FFN_ROWS = 512
FFN_CHUNK = 256
ATTN_TQ = 512
ATTN_TK = 512
MASK_VALUE = -1e30
LOG2_E = 1.4426950408889634
SKIP_LOG2 = 152.0

BF16 = jnp.bfloat16
F32 = jnp.float32


def _params(n_grid):
    return pltpu.CompilerParams(
        dimension_semantics=("arbitrary",) * n_grid,
        vmem_limit_bytes=VMEM_LIMIT_BYTES)


def _resident(shape):
    nd = len(shape)
    return pl.BlockSpec(shape, lambda *_: (0,) * nd, pipeline_mode=pl.Buffered(1))


def _rms(x, g):
    return x * lax.rsqrt(jnp.mean(x * x, axis=-1, keepdims=True) + RMS_EPS) * g


def _ffn_kernel(*refs, has_pre, has_proj, proj_scale, final_norm):
    it = iter(refs)
    x_ref = next(it)
    if has_pre:
        a_ref, wpre_ref = next(it), next(it)
    g_ref, wg_ref, wu_ref, wd_ref = next(it), next(it), next(it), next(it)
    if has_proj:
        pg_ref, pw_ref = next(it), next(it)
    if final_norm:
        fg_ref = next(it)
    o_ref = next(it)
    if has_proj:
        po_ref = next(it)
    h_ref, acc_ref = next(it), next(it)

    x = x_ref[...]
    if has_pre:
        x = x + jnp.dot(a_ref[...], wpre_ref[...], preferred_element_type=F32)
    o_ref[...] = x
    h_ref[...] = _rms(x, g_ref[...]).astype(BF16)
    for c in range(D_FF // FFN_CHUNK):
        cols = slice(c * FFN_CHUNK, (c + 1) * FFN_CHUNK)
        h = h_ref[...]
        gate = jnp.dot(h, wg_ref[:, cols], preferred_element_type=F32)
        up = jnp.dot(h, wu_ref[:, cols], preferred_element_type=F32)
        a = (gate * jax.nn.sigmoid(gate) * up).astype(BF16)
        part = jnp.dot(a, wd_ref[cols, :], preferred_element_type=F32)
        if c == 0:
            acc_ref[...] = part
        else:
            acc_ref[...] += part
    y = o_ref[...] + 0.5 * acc_ref[...]
    if has_proj:
        hp = _rms(y, pg_ref[...]).astype(BF16)
        po_ref[...] = (jnp.dot(hp, pw_ref[...], preferred_element_type=F32) * proj_scale).astype(BF16)
    if final_norm:
        y = _rms(y, fg_ref[...])
    o_ref[...] = y


def _ffn(x, g, w_gate, w_up, w_down, pre=None, proj=None, final_g=None):
    t, d = x.shape
    row = pl.BlockSpec((FFN_ROWS, d), lambda i: (i, 0))
    args, in_specs = [x], [row]
    if pre is not None:
        a, w_pre = pre
        args += [a, w_pre.astype(BF16)]
        in_specs += [pl.BlockSpec((FFN_ROWS, a.shape[1]), lambda i: (i, 0)), _resident(w_pre.shape)]
    args += [g.reshape(1, d), w_gate.astype(BF16), w_up.astype(BF16), w_down.astype(BF16)]
    in_specs += [_resident((1, d)), _resident(w_gate.shape), _resident(w_up.shape), _resident(w_down.shape)]
    out_shape, out_specs = [jax.ShapeDtypeStruct((t, d), F32)], [row]
    proj_scale = 1.0
    if proj is not None:
        pg, pw, proj_scale = proj
        args += [pg.reshape(1, d), pw.astype(BF16)]
        in_specs += [_resident((1, d)), _resident(pw.shape)]
        out_shape.append(jax.ShapeDtypeStruct((t, pw.shape[1]), BF16))
        out_specs.append(pl.BlockSpec((FFN_ROWS, pw.shape[1]), lambda i: (i, 0)))
    if final_g is not None:
        args.append(final_g.reshape(1, d))
        in_specs.append(_resident((1, d)))
    outs = pl.pallas_call(
        functools.partial(_ffn_kernel, has_pre=pre is not None, has_proj=proj is not None,
                          proj_scale=proj_scale, final_norm=final_g is not None),
        grid=(t // FFN_ROWS,),
        in_specs=in_specs,
        out_specs=out_specs,
        out_shape=out_shape,
        scratch_shapes=[pltpu.VMEM((FFN_ROWS, d), BF16), pltpu.VMEM((FFN_ROWS, d), F32)],
        compiler_params=_params(1),
        name="ffn",
    )(*args)
    return outs if proj is not None else outs[0]


def _gelu(z):
    return 0.5 * z * (1.0 + lax.erf(z * (1.0 / math.sqrt(2.0))))


def _gmlp_kernel(x_ref, g_ref, win_ref, lng_ref, lnb_ref, ws_ref, bs_ref, wout_ref, o_ref,
                 h_ref, v_ref, y_ref, *, rows):
    h_ref[...] = _rms(x_ref[...], g_ref[...]).astype(BF16)
    v = _gelu(jnp.dot(h_ref[...], win_ref[:, GM_HALF:], preferred_element_type=F32))
    mu = jnp.mean(v, axis=-1, keepdims=True)
    vc = v - mu
    var = jnp.mean(vc * vc, axis=-1, keepdims=True)
    v_ref[...] = (vc * lax.rsqrt(var + LN_EPS) * lng_ref[...] + lnb_ref[...]).astype(BF16)
    u = _gelu(jnp.dot(h_ref[...], win_ref[:, :GM_HALF], preferred_element_type=F32))
    for n in range(rows // CHUNK):
        r = slice(n * CHUNK, (n + 1) * CHUNK)
        for grp in range(GM_GROUPS):
            c = slice(grp * CHUNK, (grp + 1) * CHUNK)
            sv = jnp.dot(ws_ref[grp], v_ref[r, c], preferred_element_type=F32) + bs_ref[grp]
            y_ref[r, c] = (u[r, c] * sv).astype(BF16)
    o_ref[...] = x_ref[...] + jnp.dot(y_ref[...], wout_ref[...], preferred_element_type=F32)


def _gmlp(x, g, w_in, ln_g, ln_b, w_s, b_s, w_out, rows=512):
    t, d = x.shape
    row = pl.BlockSpec((rows, d), lambda i: (i, 0))
    causal = jnp.tril(jnp.ones((CHUNK, CHUNK), dtype=bool))
    ws = jnp.where(causal[None], w_s, jnp.zeros_like(w_s)).astype(BF16)
    bs = jnp.broadcast_to(b_s[:, :, None], (GM_GROUPS, CHUNK, CHUNK)).astype(F32)
    args = [x, g.reshape(1, d), w_in.astype(BF16), ln_g.reshape(1, GM_HALF), ln_b.reshape(1, GM_HALF),
            ws, bs, w_out.astype(BF16)]
    in_specs = [row] + [_resident(a.shape) for a in args[1:]]
    return pl.pallas_call(
        functools.partial(_gmlp_kernel, rows=rows),
        grid=(t // rows,),
        in_specs=in_specs,
        out_specs=row,
        out_shape=jax.ShapeDtypeStruct((t, d), F32),
        scratch_shapes=[pltpu.VMEM((rows, d), BF16), pltpu.VMEM((rows, GM_HALF), BF16),
                        pltpu.VMEM((rows, GM_HALF), BF16)],
        compiler_params=_params(1),
        name="gmlp",
    )(*args)


def _split_bf16(x, parts):
    out = []
    for _ in range(parts):
        t = x.astype(BF16).astype(F32)
        out.append(t)
        x = x - t
    return out


def _lane_select(lane, values):
    row = jnp.zeros(lane.shape, F32)
    for i, val in enumerate(values):
        row = jnp.where(lane == i, val, row)
    return row


def _half_norms_sq(x):
    ln = lax.broadcasted_iota(jnp.int32, x.shape, 1)
    x2 = x * x
    left = jnp.sum(jnp.where(ln < DA_HEAD_DIM, x2, 0.0), axis=-1, keepdims=True)
    right = jnp.sum(jnp.where(ln >= DA_HEAD_DIM, x2, 0.0), axis=-1, keepdims=True)
    return jnp.maximum(left, right)


def _attn_kernel(slopes_ref, q_ref, k_ref, v_ref, lq1_ref, lk1_ref, lq2_ref, lk2_ref, subg_ref, o_ref,
                 kx_ref, qx_ref, s0_ref, s1_ref, p0_ref, p1_ref, a0_ref, a1_ref, m_ref, l_ref, acc_ref, mask_ref,
                 knorm_ref, *, lambda_init, seq):
    hd = pl.program_id(1)
    qi = pl.program_id(2)
    slope = slopes_ref[hd] * LOG2_E
    lane = lax.broadcasted_iota(jnp.int32, (8, HEAD_W), 1)

    @pl.when(qi == 0)
    def _():
        rel = (lax.broadcasted_iota(jnp.int32, (ATTN_TQ, ATTN_TK), 0)
               - lax.broadcasted_iota(jnp.int32, (ATTN_TQ, ATTN_TK), 1))
        mask_ref[...] = jnp.where(rel >= 0, 0.0, MASK_VALUE)
        kx_ref[:, :HEAD_W] = k_ref[...]

        def positions(c, carry):
            start = pl.multiple_of(c * ATTN_TK, ATTN_TK)
            pos = start + lax.broadcasted_iota(jnp.int32, (ATTN_TK, HEAD_W), 0)
            ln = lax.broadcasted_iota(jnp.int32, (ATTN_TK, HEAD_W), 1)
            hi = (pos >> 7).astype(F32)
            lo = (pos & 127).astype(F32)
            cols = jnp.where(ln >= 8, 0.0, jnp.where(ln >= 6, 1.0, jnp.where(ln % 2 == 0, hi, lo)))
            kx_ref[pl.ds(start, ATTN_TK), HEAD_W:] = cols.astype(BF16)
            return carry

        lax.fori_loop(0, seq // ATTN_TK, positions, 0)

        def key_norms(c, best):
            start = pl.multiple_of(c * ATTN_TK, ATTN_TK)
            k = k_ref[pl.ds(start, ATTN_TK), :].astype(F32)
            return jnp.maximum(best, _half_norms_sq(k))

        best = lax.fori_loop(0, seq // ATTN_TK, key_norms, jnp.zeros((ATTN_TK, 1), F32))
        knorm_ref[...] = jnp.broadcast_to(jnp.max(best, axis=0, keepdims=True), knorm_ref.shape)

    q = q_ref[...]
    ql = lax.broadcasted_iota(jnp.int32, q.shape, 1)
    qx_ref[:ATTN_TQ, :HEAD_W] = jnp.where(ql < DA_HEAD_DIM, q, jnp.zeros_like(q))
    qx_ref[ATTN_TQ:, :HEAD_W] = jnp.where(ql >= DA_HEAD_DIM, q, jnp.zeros_like(q))
    svec = jnp.full((8, HEAD_W), slope, F32)
    parts = _split_bf16(svec, 3)
    base = _split_bf16(-svec * (qi * ATTN_TQ).astype(F32), 2)
    coef = _lane_select(lane, [parts[0] * 128.0, parts[0], parts[1] * 128.0, parts[1],
                               parts[2] * 128.0, parts[2], base[0], base[1]])
    qx_ref[:, HEAD_W:] = jnp.broadcast_to(coef[:1], (2 * ATTN_TQ, HEAD_W)).astype(BF16)
    m_ref[...] = jnp.full(m_ref.shape, MASK_VALUE, F32)
    l_ref[...] = jnp.zeros(l_ref.shape, F32)
    acc_ref[...] = jnp.zeros(acc_ref.shape, F32)

    def scores(kj, s_ref):
        start = pl.multiple_of(kj * ATTN_TK, ATTN_TK)
        k = kx_ref[pl.ds(start, ATTN_TK), :]
        s_ref[...] = lax.dot_general(qx_ref[...], k, (((1,), (1,)), ((), ())), preferred_element_type=F32)

    def accumulate(kj, p_ref, alpha_ref):
        start = pl.multiple_of(kj * ATTN_TK, ATTN_TK)
        v = v_ref[pl.ds(start, ATTN_TK), :]
        acc_ref[...] = alpha_ref[...] * acc_ref[...] + jnp.dot(p_ref[...], v, preferred_element_type=F32)

    def softmax(s_ref, p_ref, alpha_ref, masked):
        def block_scores():
            if masked:
                return s_ref[...] + jnp.concatenate([mask_ref[...], mask_ref[...]], axis=0)
            return s_ref[...]
        m_old = m_ref[...]
        m_new = jnp.maximum(m_old, jnp.max(block_scores(), axis=-1, keepdims=True))
        alpha = jnp.exp2(m_old - m_new)
        alpha_ref[...] = alpha
        m_ref[...] = m_new
        s = block_scores()
        ps = [jnp.exp2(s[:, j * HEAD_W:(j + 1) * HEAD_W] - m_new) for j in range(ATTN_TK // HEAD_W)]
        l_ref[...] = alpha * l_ref[...] + functools.reduce(lambda a, b: a + b, ps)
        p_ref[...] = jnp.concatenate(ps, axis=1).astype(BF16)

    qnorm = jnp.max(_half_norms_sq(q.astype(F32)), axis=0, keepdims=True)
    bound = jnp.sqrt(qnorm * knorm_ref[:1, :1]) * 1.01 + 1.0
    reach = (2.0 * bound + SKIP_LOG2) / slope
    first = jnp.floor(((qi * ATTN_TQ).astype(F32) - reach) * (1.0 / ATTN_TK))
    n_far = qi - jnp.clip(first, 0.0, qi.astype(F32)).astype(jnp.int32)[0, 0]

    s_bufs, p_bufs, a_bufs = (s0_ref, s1_ref), (p0_ref, p1_ref), (a0_ref, a1_ref)

    def step(t, par, last=False):
        accumulate(qi - (t - 1), p_bufs[1 - par], a_bufs[1 - par])
        softmax(s_bufs[par], p_bufs[par], a_bufs[par], masked=False)
        if not last:
            scores(qi - jnp.minimum(t + 1, n_far), s_bufs[1 - par])

    scores(qi, s0_ref)
    softmax(s0_ref, p0_ref, a0_ref, masked=True)
    scores(qi - jnp.minimum(1, n_far), s1_ref)

    def steps(first, count):
        for i in range(count):
            step(first + i, (i + 1) % 2)

    def quad(u, carry):
        steps(4 * u + 1, 4)
        return carry

    def pair(u, carry):
        steps(done + 2 * u + 1, 2)
        return carry

    done = (n_far // 4) * 4
    lax.fori_loop(0, n_far // 4, quad, 0)
    lax.fori_loop(0, (n_far - done) // 2, pair, 0)

    @pl.when(n_far % 2 == 1)
    def _():
        step(n_far, 1, last=True)
        accumulate(qi - n_far, p1_ref, a1_ref)

    @pl.when(n_far % 2 == 0)
    def _():
        accumulate(qi - n_far, p0_ref, a0_ref)

    lam = (jnp.exp(jnp.sum(lq1_ref[...] * lk1_ref[...], axis=-1, keepdims=True))
           - jnp.exp(jnp.sum(lq2_ref[...] * lk2_ref[...], axis=-1, keepdims=True))
           + lambda_init)
    l = jnp.sum(l_ref[...], axis=-1, keepdims=True)
    o = acc_ref[...] / l
    o = o[:ATTN_TQ] - lam * o[ATTN_TQ:]
    o_ref[...] = (_rms(o, subg_ref[...]) * (1.0 - lambda_init)).astype(BF16)


def _attention(q, kv, slopes, lam_q1, lam_k1, lam_q2, lam_k2, subln_g, batch, seq, lambda_init):
    nq = seq // ATTN_TQ
    vec = lambda a: a.reshape(1, -1).astype(F32)
    lam_spec = _resident((1, DA_HEAD_DIM))
    return pl.pallas_call(
        functools.partial(_attn_kernel, lambda_init=lambda_init, seq=seq),
        grid=(batch, DA_HEADS, nq),
        in_specs=[
            pl.BlockSpec(memory_space=pltpu.SMEM),
            pl.BlockSpec((None, ATTN_TQ, HEAD_W), lambda b, h, i: (b, i, h)),
            pl.BlockSpec((None, seq, HEAD_W), lambda b, h, i: (b, 0, h)),
            pl.BlockSpec((None, seq, HEAD_W), lambda b, h, i: (b, 0, DA_HEADS + h)),
            lam_spec, lam_spec, lam_spec, lam_spec,
            _resident((1, HEAD_W)),
        ],
        out_specs=pl.BlockSpec((None, ATTN_TQ, HEAD_W), lambda b, h, i: (b, i, h)),
        out_shape=jax.ShapeDtypeStruct((batch, seq, DA_HEADS * HEAD_W), BF16),
        scratch_shapes=[
            pltpu.VMEM((seq, 2 * HEAD_W), BF16),
            pltpu.VMEM((2 * ATTN_TQ, 2 * HEAD_W), BF16),
            pltpu.VMEM((2 * ATTN_TQ, ATTN_TK), F32),
            pltpu.VMEM((2 * ATTN_TQ, ATTN_TK), F32),
            pltpu.VMEM((2 * ATTN_TQ, ATTN_TK), BF16),
            pltpu.VMEM((2 * ATTN_TQ, ATTN_TK), BF16),
            pltpu.VMEM((2 * ATTN_TQ, HEAD_W), F32),
            pltpu.VMEM((2 * ATTN_TQ, HEAD_W), F32),
            pltpu.VMEM((2 * ATTN_TQ, HEAD_W), F32),
            pltpu.VMEM((2 * ATTN_TQ, HEAD_W), F32),
            pltpu.VMEM((2 * ATTN_TQ, HEAD_W), F32),
            pltpu.VMEM((ATTN_TQ, ATTN_TK), F32),
            pltpu.VMEM((8, HEAD_W), F32),
        ],
        compiler_params=_params(3),
        name="diff_attn",
    )(slopes, q, kv, kv, vec(lam_q1), vec(lam_k1), vec(lam_q2), vec(lam_k2), vec(subln_g))


def kernel(x, ffn_norm1_g, ffn1_w_gate, ffn1_w_up, ffn1_w_down, mix_norm_g, ffn_norm2_g, ffn2_w_gate, ffn2_w_up, ffn2_w_down, gm_w_in, gm_ln_g, gm_ln_b, gm_w_s, gm_b_s, gm_w_out, kv_norm_g, w_k, w_v, da_w_q, da_lam_q1, da_lam_k1, da_lam_q2, da_lam_k2, da_subln_g, da_w_o, final_norm_g):
    batch, seq, d = x.shape
    depth = ffn_norm1_g.shape[0]
    assert (d, depth, N_A_LAYERS) == (D_MODEL, 2, 1)
    xs = x.reshape(batch * seq, d)

    def ffn1(l, x_in, **fused):
        return _ffn(x_in, ffn_norm1_g[l], ffn1_w_gate[l], ffn1_w_up[l], ffn1_w_down[l], **fused)

    def ffn2(l, x_in, **fused):
        return _ffn(x_in, ffn_norm2_g[l], ffn2_w_gate[l], ffn2_w_up[l], ffn2_w_down[l], **fused)

    xs = ffn1(0, xs)
    xs = _gmlp(xs, mix_norm_g[0], gm_w_in[0], gm_ln_g[0], gm_ln_b[0], gm_w_s[0], gm_b_s[0], gm_w_out[0])
    xs, kv = ffn2(0, xs, proj=(kv_norm_g, jnp.concatenate([w_k, w_v], axis=1), 1.0))

    lambda_init = 0.8 - 0.6 * math.exp(-0.3 * 1)
    xs, q = ffn1(1, xs, proj=(mix_norm_g[1], da_w_q[0], (DA_HEAD_DIM ** -0.5) * LOG2_E))
    slopes = jnp.exp2(-8.0 * (jnp.arange(DA_HEADS, dtype=F32) + 1.0) / DA_HEADS)
    a = _attention(q.reshape(batch, seq, -1), kv.reshape(batch, seq, -1), slopes,
                   da_lam_q1[0], da_lam_k1[0], da_lam_q2[0], da_lam_k2[0], da_subln_g[0],
                   batch, seq, lambda_init)
    xs = ffn2(1, xs, pre=(a.reshape(batch * seq, -1), da_w_o[0]), final_g=final_norm_g)
    return xs.reshape(batch, seq, d)
```

```python
import functools
import math

import jax
import jax.numpy as jnp
from jax import lax
from jax.experimental import pallas as pl
from jax.experimental.pallas import tpu as pltpu

D_MODEL = 1024
D_FF = 2816
GM_HALF = 1024
GM_GROUPS = 8
CHUNK = 128
DA_HEADS = 8
DA_HEAD_DIM = 64
HEAD_W = 2 * DA_HEAD_DIM
RMS_EPS = 1e-6
LN_EPS = 1e-5
N_A_LAYERS = 1

VMEM_LIMIT_BYTES = 56 * 1024 * 1024

FFN_ROWS = 512
FFN_CHUNK = 256
ATTN_TQ = 512
ATTN_TK = 512
MASK_VALUE = -1e30
LOG2_E = 1.4426950408889634
NORM_ROWS = 2048
SKIP_LOG2 = 152.0

BF16 = jnp.bfloat16
F32 = jnp.float32


def _params(n_grid):
    return pltpu.CompilerParams(
        dimension_semantics=("arbitrary",) * n_grid,
        vmem_limit_bytes=VMEM_LIMIT_BYTES)


def _resident(shape):
    nd = len(shape)
    return pl.BlockSpec(shape, lambda *_: (0,) * nd, pipeline_mode=pl.Buffered(1))


def _rms(x, g):
    return x * lax.rsqrt(jnp.mean(x * x, axis=-1, keepdims=True) + RMS_EPS) * g


def _ffn_kernel(*refs, has_pre, has_proj, proj_scale, final_norm):
    it = iter(refs)
    x_ref = next(it)
    if has_pre:
        a_ref, wpre_ref = next(it), next(it)
    g_ref, wg_ref, wu_ref, wd_ref = next(it), next(it), next(it), next(it)
    if has_proj:
        pg_ref, pw_ref = next(it), next(it)
    if final_norm:
        fg_ref = next(it)
    o_ref = next(it)
    if has_proj:
        po_ref = next(it)
    h_ref, acc_ref = next(it), next(it)

    x = x_ref[...]
    if has_pre:
        x = x + jnp.dot(a_ref[...], wpre_ref[...], preferred_element_type=F32)
    o_ref[...] = x
    h_ref[...] = _rms(x, g_ref[...]).astype(BF16)
    for c in range(D_FF // FFN_CHUNK):
        cols = slice(c * FFN_CHUNK, (c + 1) * FFN_CHUNK)
        h = h_ref[...]
        gate = jnp.dot(h, wg_ref[:, cols], preferred_element_type=F32)
        up = jnp.dot(h, wu_ref[:, cols], preferred_element_type=F32)
        a = (gate * jax.nn.sigmoid(gate) * up).astype(BF16)
        part = jnp.dot(a, wd_ref[cols, :], preferred_element_type=F32)
        if c == 0:
            acc_ref[...] = part
        else:
            acc_ref[...] += part
    y = o_ref[...] + 0.5 * acc_ref[...]
    if has_proj:
        hp = _rms(y, pg_ref[...]).astype(BF16)
        po_ref[...] = (jnp.dot(hp, pw_ref[...], preferred_element_type=F32) * proj_scale).astype(BF16)
    if final_norm:
        y = _rms(y, fg_ref[...])
    o_ref[...] = y


def _ffn(x, g, w_gate, w_up, w_down, pre=None, proj=None, final_g=None):
    t, d = x.shape
    row = pl.BlockSpec((FFN_ROWS, d), lambda i: (i, 0))
    args, in_specs = [x], [row]
    if pre is not None:
        a, w_pre = pre
        args += [a, w_pre.astype(BF16)]
        in_specs += [pl.BlockSpec((FFN_ROWS, a.shape[1]), lambda i: (i, 0)), _resident(w_pre.shape)]
    args += [g.reshape(1, d), w_gate.astype(BF16), w_up.astype(BF16), w_down.astype(BF16)]
    in_specs += [_resident((1, d)), _resident(w_gate.shape), _resident(w_up.shape), _resident(w_down.shape)]
    out_shape, out_specs = [jax.ShapeDtypeStruct((t, d), F32)], [row]
    proj_scale = 1.0
    if proj is not None:
        pg, pw, proj_scale = proj
        args += [pg.reshape(1, d), pw.astype(BF16)]
        in_specs += [_resident((1, d)), _resident(pw.shape)]
        out_shape.append(jax.ShapeDtypeStruct((t, pw.shape[1]), BF16))
        out_specs.append(pl.BlockSpec((FFN_ROWS, pw.shape[1]), lambda i: (i, 0)))
    if final_g is not None:
        args.append(final_g.reshape(1, d))
        in_specs.append(_resident((1, d)))
    outs = pl.pallas_call(
        functools.partial(_ffn_kernel, has_pre=pre is not None, has_proj=proj is not None,
                          proj_scale=proj_scale, final_norm=final_g is not None),
        grid=(t // FFN_ROWS,),
        in_specs=in_specs,
        out_specs=out_specs,
        out_shape=out_shape,
        scratch_shapes=[pltpu.VMEM((FFN_ROWS, d), BF16), pltpu.VMEM((FFN_ROWS, d), F32)],
        compiler_params=_params(1),
        name="ffn",
    )(*args)
    return outs if proj is not None else outs[0]


def _gelu(z):
    return 0.5 * z * (1.0 + lax.erf(z * (1.0 / math.sqrt(2.0))))


def _gmlp_kernel(x_ref, g_ref, win_ref, lng_ref, lnb_ref, ws_ref, bs_ref, wout_ref, o_ref,
                 h_ref, v_ref, y_ref, *, rows):
    h_ref[...] = _rms(x_ref[...], g_ref[...]).astype(BF16)
    v = _gelu(jnp.dot(h_ref[...], win_ref[:, GM_HALF:], preferred_element_type=F32))
    mu = jnp.mean(v, axis=-1, keepdims=True)
    vc = v - mu
    var = jnp.mean(vc * vc, axis=-1, keepdims=True)
    v_ref[...] = (vc * lax.rsqrt(var + LN_EPS) * lng_ref[...] + lnb_ref[...]).astype(BF16)
    u = _gelu(jnp.dot(h_ref[...], win_ref[:, :GM_HALF], preferred_element_type=F32))
    for n in range(rows // CHUNK):
        r = slice(n * CHUNK, (n + 1) * CHUNK)
        for grp in range(GM_GROUPS):
            c = slice(grp * CHUNK, (grp + 1) * CHUNK)
            sv = jnp.dot(ws_ref[grp], v_ref[r, c], preferred_element_type=F32) + bs_ref[grp]
            y_ref[r, c] = (u[r, c] * sv).astype(BF16)
    o_ref[...] = x_ref[...] + jnp.dot(y_ref[...], wout_ref[...], preferred_element_type=F32)


def _gmlp(x, g, w_in, ln_g, ln_b, w_s, b_s, w_out, rows=512):
    t, d = x.shape
    row = pl.BlockSpec((rows, d), lambda i: (i, 0))
    causal = jnp.tril(jnp.ones((CHUNK, CHUNK), dtype=bool))
    ws = jnp.where(causal[None], w_s, jnp.zeros_like(w_s)).astype(BF16)
    bs = jnp.broadcast_to(b_s[:, :, None], (GM_GROUPS, CHUNK, CHUNK)).astype(F32)
    args = [x, g.reshape(1, d), w_in.astype(BF16), ln_g.reshape(1, GM_HALF), ln_b.reshape(1, GM_HALF),
            ws, bs, w_out.astype(BF16)]
    in_specs = [row] + [_resident(a.shape) for a in args[1:]]
    return pl.pallas_call(
        functools.partial(_gmlp_kernel, rows=rows),
        grid=(t // rows,),
        in_specs=in_specs,
        out_specs=row,
        out_shape=jax.ShapeDtypeStruct((t, d), F32),
        scratch_shapes=[pltpu.VMEM((rows, d), BF16), pltpu.VMEM((rows, GM_HALF), BF16),
                        pltpu.VMEM((rows, GM_HALF), BF16)],
        compiler_params=_params(1),
        name="gmlp",
    )(*args)


def _split_bf16(x, parts):
    out = []
    for _ in range(parts):
        t = x.astype(BF16).astype(F32)
        out.append(t)
        x = x - t
    return out


def _lane_select(lane, values):
    row = jnp.zeros(lane.shape, F32)
    for i, val in enumerate(values):
        row = jnp.where(lane == i, val, row)
    return row


def _half_norms_sq(x):
    ln = lax.broadcasted_iota(jnp.int32, x.shape, 1)
    x2 = x * x
    left = jnp.sum(jnp.where(ln < DA_HEAD_DIM, x2, 0.0), axis=-1, keepdims=True)
    right = jnp.sum(jnp.where(ln >= DA_HEAD_DIM, x2, 0.0), axis=-1, keepdims=True)
    return jnp.maximum(left, right)


def _attn_kernel(slopes_ref, q_ref, k_ref, v_ref, lq1_ref, lk1_ref, lq2_ref, lk2_ref, subg_ref, o_ref,
                 kx_ref, qx_ref, s0_ref, s1_ref, p0_ref, p1_ref, a0_ref, a1_ref, m_ref, l_ref, acc_ref, mask_ref,
                 knorm_ref, *, lambda_init, seq):
    hd = pl.program_id(1)
    qi = pl.program_id(2)
    slope = slopes_ref[hd] * LOG2_E
    lane = lax.broadcasted_iota(jnp.int32, (8, HEAD_W), 1)

    @pl.when(qi == 0)
    def _():
        rel = (lax.broadcasted_iota(jnp.int32, (ATTN_TQ, ATTN_TK), 0)
               - lax.broadcasted_iota(jnp.int32, (ATTN_TQ, ATTN_TK), 1))
        mask_ref[...] = jnp.where(rel >= 0, 0.0, MASK_VALUE)
        kx_ref[:, :HEAD_W] = k_ref[...]

        def key_norms(c, best):
            start = pl.multiple_of(c * NORM_ROWS, NORM_ROWS)
            k = k_ref[pl.ds(start, NORM_ROWS), :].astype(F32)
            n2 = jnp.dot((k * k).astype(BF16), halves, preferred_element_type=F32)
            return jnp.maximum(best, jnp.max(n2, axis=0, keepdims=True))

        dd = lax.broadcasted_iota(jnp.int32, (HEAD_W, HEAD_W), 0)
        cc = lax.broadcasted_iota(jnp.int32, (HEAD_W, HEAD_W), 1)
        halves = jnp.where((dd < DA_HEAD_DIM) == (cc % 2 == 0), 1.0, 0.0).astype(BF16)
        best = lax.fori_loop(0, seq // NORM_ROWS, key_norms, jnp.zeros((1, HEAD_W), F32))
        knorm_ref[...] = jnp.broadcast_to(jnp.max(best, axis=1, keepdims=True), knorm_ref.shape)

    @pl.when((pl.program_id(0) == 0) & (hd == 0) & (qi == 0))
    def _():
        def positions(c, carry):
            start = pl.multiple_of(c * ATTN_TK, ATTN_TK)
            pos = start + lax.broadcasted_iota(jnp.int32, (ATTN_TK, HEAD_W), 0)
            ln = lax.broadcasted_iota(jnp.int32, (ATTN_TK, HEAD_W), 1)
            hi = (pos >> 7).astype(F32)
            lo = (pos & 127).astype(F32)
            cols = jnp.where(ln >= 8, 0.0, jnp.where(ln >= 6, 1.0, jnp.where(ln % 2 == 0, hi, lo)))
            kx_ref[pl.ds(start, ATTN_TK), HEAD_W:] = cols.astype(BF16)
            return carry

        lax.fori_loop(0, seq // ATTN_TK, positions, 0)

    q = q_ref[...]
    ql = lax.broadcasted_iota(jnp.int32, q.shape, 1)
    qx_ref[:ATTN_TQ, :HEAD_W] = jnp.where(ql < DA_HEAD_DIM, q, jnp.zeros_like(q))
    qx_ref[ATTN_TQ:, :HEAD_W] = jnp.where(ql >= DA_HEAD_DIM, q, jnp.zeros_like(q))
    svec = jnp.full((8, HEAD_W), slope, F32)
    parts = _split_bf16(svec, 3)
    base = _split_bf16(-svec * (qi * ATTN_TQ).astype(F32), 2)
    coef = _lane_select(lane, [parts[0] * 128.0, parts[0], parts[1] * 128.0, parts[1],
                               parts[2] * 128.0, parts[2], base[0], base[1]])
    qx_ref[:, HEAD_W:] = jnp.broadcast_to(coef[:1], (2 * ATTN_TQ, HEAD_W)).astype(BF16)
    m_ref[...] = jnp.full(m_ref.shape, MASK_VALUE, F32)
    l_ref[...] = jnp.zeros(l_ref.shape, F32)
    acc_ref[...] = jnp.zeros(acc_ref.shape, F32)

    def scores(kj, s_ref):
        start = pl.multiple_of(kj * ATTN_TK, ATTN_TK)
        k = kx_ref[pl.ds(start, ATTN_TK), :]
        s_ref[...] = lax.dot_general(qx_ref[...], k, (((1,), (1,)), ((), ())), preferred_element_type=F32)

    def accumulate(kj, p_ref, alpha_ref):
        start = pl.multiple_of(kj * ATTN_TK, ATTN_TK)
        v = v_ref[pl.ds(start, ATTN_TK), :]
        acc_ref[...] = alpha_ref[...] * acc_ref[...] + jnp.dot(p_ref[...], v, preferred_element_type=F32)

    def softmax(s_ref, p_ref, alpha_ref, masked):
        def block_scores():
            if masked:
                return s_ref[...] + jnp.concatenate([mask_ref[...], mask_ref[...]], axis=0)
            return s_ref[...]
        m_old = m_ref[...]
        m_new = jnp.maximum(m_old, jnp.max(block_scores(), axis=-1, keepdims=True))
        alpha = jnp.exp2(m_old - m_new)
        alpha_ref[...] = alpha
        m_ref[...] = m_new
        s = block_scores()
        ps = [jnp.exp2(s[:, j * HEAD_W:(j + 1) * HEAD_W] - m_new) for j in range(ATTN_TK // HEAD_W)]
        l_ref[...] = alpha * l_ref[...] + functools.reduce(lambda a, b: a + b, ps)
        p_ref[...] = jnp.concatenate(ps, axis=1).astype(BF16)

    qnorm = jnp.max(_half_norms_sq(q.astype(F32)), axis=0, keepdims=True)
    bound = jnp.sqrt(qnorm * knorm_ref[:1, :1]) * 1.02 + 1.0
    reach = (2.0 * bound + SKIP_LOG2) / slope
    first = jnp.floor(((qi * ATTN_TQ).astype(F32) - reach) * (1.0 / ATTN_TK))
    n_far = qi - jnp.clip(first, 0.0, qi.astype(F32)).astype(jnp.int32)[0, 0]

    s_bufs, p_bufs, a_bufs = (s0_ref, s1_ref), (p0_ref, p1_ref), (a0_ref, a1_ref)

    def step(t, par, last=False):
        accumulate(qi - (t - 1), p_bufs[1 - par], a_bufs[1 - par])
        softmax(s_bufs[par], p_bufs[par], a_bufs[par], masked=False)
        if not last:
            scores(qi - jnp.minimum(t + 1, n_far), s_bufs[1 - par])

    scores(qi, s0_ref)
    softmax(s0_ref, p0_ref, a0_ref, masked=True)
    scores(qi - jnp.minimum(1, n_far), s1_ref)

    def steps(first, count):
        for i in range(count):
            step(first + i, (i + 1) % 2)

    def octet(u, carry):
        steps(8 * u + 1, 8)
        return carry

    def quad(u, carry):
        steps(done8 + 4 * u + 1, 4)
        return carry

    def pair(u, carry):
        steps(done + 2 * u + 1, 2)
        return carry

    done8 = (n_far // 8) * 8
    done = (n_far // 4) * 4
    lax.fori_loop(0, n_far // 8, octet, 0)
    lax.fori_loop(0, (n_far - done8) // 4, quad, 0)
    lax.fori_loop(0, (n_far - done) // 2, pair, 0)

    @pl.when(n_far % 2 == 1)
    def _():
        step(n_far, 1, last=True)
        accumulate(qi - n_far, p1_ref, a1_ref)

    @pl.when(n_far % 2 == 0)
    def _():
        accumulate(qi - n_far, p0_ref, a0_ref)

    lam = (jnp.exp(jnp.sum(lq1_ref[...] * lk1_ref[...], axis=-1, keepdims=True))
           - jnp.exp(jnp.sum(lq2_ref[...] * lk2_ref[...], axis=-1, keepdims=True))
           + lambda_init)
    l = jnp.sum(l_ref[...], axis=-1, keepdims=True)
    o = acc_ref[...] / l
    o = o[:ATTN_TQ] - lam * o[ATTN_TQ:]
    o_ref[...] = (_rms(o, subg_ref[...]) * (1.0 - lambda_init)).astype(BF16)


def _attention(q, kv, slopes, lam_q1, lam_k1, lam_q2, lam_k2, subln_g, batch, seq, lambda_init):
    nq = seq // ATTN_TQ
    vec = lambda a: a.reshape(1, -1).astype(F32)
    lam_spec = _resident((1, DA_HEAD_DIM))
    return pl.pallas_call(
        functools.partial(_attn_kernel, lambda_init=lambda_init, seq=seq),
        grid=(batch, DA_HEADS, nq),
        in_specs=[
            pl.BlockSpec(memory_space=pltpu.SMEM),
            pl.BlockSpec((None, ATTN_TQ, HEAD_W), lambda b, h, i: (b, i, h)),
            pl.BlockSpec((None, seq, HEAD_W), lambda b, h, i: (b, 0, h)),
            pl.BlockSpec((None, seq, HEAD_W), lambda b, h, i: (b, 0, DA_HEADS + h)),
            lam_spec, lam_spec, lam_spec, lam_spec,
            _resident((1, HEAD_W)),
        ],
        out_specs=pl.BlockSpec((None, ATTN_TQ, HEAD_W), lambda b, h, i: (b, i, h)),
        out_shape=jax.ShapeDtypeStruct((batch, seq, DA_HEADS * HEAD_W), BF16),
        scratch_shapes=[
            pltpu.VMEM((seq, 2 * HEAD_W), BF16),
            pltpu.VMEM((2 * ATTN_TQ, 2 * HEAD_W), BF16),
            pltpu.VMEM((2 * ATTN_TQ, ATTN_TK), F32),
            pltpu.VMEM((2 * ATTN_TQ, ATTN_TK), F32),
            pltpu.VMEM((2 * ATTN_TQ, ATTN_TK), BF16),
            pltpu.VMEM((2 * ATTN_TQ, ATTN_TK), BF16),
            pltpu.VMEM((2 * ATTN_TQ, HEAD_W), F32),
            pltpu.VMEM((2 * ATTN_TQ, HEAD_W), F32),
            pltpu.VMEM((2 * ATTN_TQ, HEAD_W), F32),
            pltpu.VMEM((2 * ATTN_TQ, HEAD_W), F32),
            pltpu.VMEM((2 * ATTN_TQ, HEAD_W), F32),
            pltpu.VMEM((ATTN_TQ, ATTN_TK), F32),
            pltpu.VMEM((8, HEAD_W), F32),
        ],
        compiler_params=_params(3),
        name="diff_attn",
    )(slopes, q, kv, kv, vec(lam_q1), vec(lam_k1), vec(lam_q2), vec(lam_k2), vec(subln_g))


def kernel(x, ffn_norm1_g, ffn1_w_gate, ffn1_w_up, ffn1_w_down, mix_norm_g, ffn_norm2_g, ffn2_w_gate, ffn2_w_up, ffn2_w_down, gm_w_in, gm_ln_g, gm_ln_b, gm_w_s, gm_b_s, gm_w_out, kv_norm_g, w_k, w_v, da_w_q, da_lam_q1, da_lam_k1, da_lam_q2, da_lam_k2, da_subln_g, da_w_o, final_norm_g):
    batch, seq, d = x.shape
    depth = ffn_norm1_g.shape[0]
    assert (d, depth, N_A_LAYERS) == (D_MODEL, 2, 1)
    xs = x.reshape(batch * seq, d)

    def ffn1(l, x_in, **fused):
        return _ffn(x_in, ffn_norm1_g[l], ffn1_w_gate[l], ffn1_w_up[l], ffn1_w_down[l], **fused)

    def ffn2(l, x_in, **fused):
        return _ffn(x_in, ffn_norm2_g[l], ffn2_w_gate[l], ffn2_w_up[l], ffn2_w_down[l], **fused)

    xs = ffn1(0, xs)
    xs = _gmlp(xs, mix_norm_g[0], gm_w_in[0], gm_ln_g[0], gm_ln_b[0], gm_w_s[0], gm_b_s[0], gm_w_out[0])
    xs, kv = ffn2(0, xs, proj=(kv_norm_g, jnp.concatenate([w_k, w_v], axis=1), 1.0))

    lambda_init = 0.8 - 0.6 * math.exp(-0.3 * 1)
    xs, q = ffn1(1, xs, proj=(mix_norm_g[1], da_w_q[0], (DA_HEAD_DIM ** -0.5) * LOG2_E))
    slopes = jnp.exp2(-8.0 * (jnp.arange(DA_HEADS, dtype=F32) + 1.0) / DA_HEADS)
    a = _attention(q.reshape(batch, seq, -1), kv.reshape(batch, seq, -1), slopes,
                   da_lam_q1[0], da_lam_k1[0], da_lam_q2[0], da_lam_k2[0], da_subln_g[0],
                   batch, seq, lambda_init)
    xs = ffn2(1, xs, pre=(a.reshape(batch * seq, -1), da_w_o[0]), final_g=final_norm_g)
    return xs.reshape(batch, seq, d)
```

```python
import functools
import math

import jax
import jax.numpy as jnp
from jax import lax
from jax.experimental import pallas as pl
from jax.experimental.pallas import tpu as pltpu

D_MODEL = 1024
D_FF = 2816
GM_HALF = 1024
GM_GROUPS = 8
CHUNK = 128
DA_HEADS = 8
DA_HEAD_DIM = 64
HEAD_W = 2 * DA_HEAD_DIM
RMS_EPS = 1e-6
LN_EPS = 1e-5
N_A_LAYERS = 1

VMEM_LIMIT_BYTES = 56 * 1024 * 1024

FFN_ROWS = 512
FFN_CHUNK = 256
ATTN_TQ = 512
ATTN_TK = 512
MASK_VALUE = -1e30
LOG2_E = 1.4426950408889634
NORM_ROWS = 2048
SKIP_LOG2 = 152.0

BF16 = jnp.bfloat16
F32 = jnp.float32


def _params(n_grid):
    return pltpu.CompilerParams(
        dimension_semantics=("arbitrary",) * n_grid,
        vmem_limit_bytes=VMEM_LIMIT_BYTES)


def _resident(shape):
    nd = len(shape)
    return pl.BlockSpec(shape, lambda *_: (0,) * nd, pipeline_mode=pl.Buffered(1))


def _rms(x, g):
    return x * lax.rsqrt(jnp.mean(x * x, axis=-1, keepdims=True) + RMS_EPS) * g


def _ffn_kernel(*refs, has_pre, has_proj, proj_scale, final_norm):
    it = iter(refs)
    x_ref = next(it)
    if has_pre:
        a_ref, wpre_ref = next(it), next(it)
    g_ref, wg_ref, wu_ref, wd_ref = next(it), next(it), next(it), next(it)
    if has_proj:
        pg_ref, pw_ref = next(it), next(it)
    if final_norm:
        fg_ref = next(it)
    o_ref = next(it)
    if has_proj:
        po_ref = next(it)
    h_ref, acc_ref = next(it), next(it)

    x = x_ref[...]
    if has_pre:
        x = x + jnp.dot(a_ref[...], wpre_ref[...], preferred_element_type=F32)
    o_ref[...] = x
    h_ref[...] = _rms(x, g_ref[...]).astype(BF16)
    for c in range(D_FF // FFN_CHUNK):
        cols = slice(c * FFN_CHUNK, (c + 1) * FFN_CHUNK)
        h = h_ref[...]
        gate = jnp.dot(h, wg_ref[:, cols], preferred_element_type=F32)
        up = jnp.dot(h, wu_ref[:, cols], preferred_element_type=F32)
        a = (gate * jax.nn.sigmoid(gate) * up).astype(BF16)
        part = jnp.dot(a, wd_ref[cols, :], preferred_element_type=F32)
        if c == 0:
            acc_ref[...] = part
        else:
            acc_ref[...] += part
    y = o_ref[...] + 0.5 * acc_ref[...]
    if has_proj:
        hp = _rms(y, pg_ref[...]).astype(BF16)
        po_ref[...] = (jnp.dot(hp, pw_ref[...], preferred_element_type=F32) * proj_scale).astype(BF16)
    if final_norm:
        y = _rms(y, fg_ref[...])
    o_ref[...] = y


def _ffn(x, g, w_gate, w_up, w_down, pre=None, proj=None, final_g=None):
    t, d = x.shape
    row = pl.BlockSpec((FFN_ROWS, d), lambda i: (i, 0))
    args, in_specs = [x], [row]
    if pre is not None:
        a, w_pre = pre
        args += [a, w_pre.astype(BF16)]
        in_specs += [pl.BlockSpec((FFN_ROWS, a.shape[1]), lambda i: (i, 0)), _resident(w_pre.shape)]
    args += [g.reshape(1, d), w_gate.astype(BF16), w_up.astype(BF16), w_down.astype(BF16)]
    in_specs += [_resident((1, d)), _resident(w_gate.shape), _resident(w_up.shape), _resident(w_down.shape)]
    out_shape, out_specs = [jax.ShapeDtypeStruct((t, d), F32)], [row]
    proj_scale = 1.0
    if proj is not None:
        pg, pw, proj_scale = proj
        args += [pg.reshape(1, d), pw.astype(BF16)]
        in_specs += [_resident((1, d)), _resident(pw.shape)]
        out_shape.append(jax.ShapeDtypeStruct((t, pw.shape[1]), BF16))
        out_specs.append(pl.BlockSpec((FFN_ROWS, pw.shape[1]), lambda i: (i, 0)))
    if final_g is not None:
        args.append(final_g.reshape(1, d))
        in_specs.append(_resident((1, d)))
    outs = pl.pallas_call(
        functools.partial(_ffn_kernel, has_pre=pre is not None, has_proj=proj is not None,
                          proj_scale=proj_scale, final_norm=final_g is not None),
        grid=(t // FFN_ROWS,),
        in_specs=in_specs,
        out_specs=out_specs,
        out_shape=out_shape,
        scratch_shapes=[pltpu.VMEM((FFN_ROWS, d), BF16), pltpu.VMEM((FFN_ROWS, d), F32)],
        compiler_params=_params(1),
        name="ffn",
    )(*args)
    return outs if proj is not None else outs[0]


def _gelu(z):
    return 0.5 * z * (1.0 + lax.erf(z * (1.0 / math.sqrt(2.0))))


def _gmlp_kernel(x_ref, g_ref, win_ref, lng_ref, lnb_ref, ws_ref, bs_ref, wout_ref, o_ref,
                 h_ref, v_ref, y_ref, *, rows):
    h_ref[...] = _rms(x_ref[...], g_ref[...]).astype(BF16)
    v = _gelu(jnp.dot(h_ref[...], win_ref[:, GM_HALF:], preferred_element_type=F32))
    mu = jnp.mean(v, axis=-1, keepdims=True)
    vc = v - mu
    var = jnp.mean(vc * vc, axis=-1, keepdims=True)
    v_ref[...] = (vc * lax.rsqrt(var + LN_EPS) * lng_ref[...] + lnb_ref[...]).astype(BF16)
    u = _gelu(jnp.dot(h_ref[...], win_ref[:, :GM_HALF], preferred_element_type=F32))
    for n in range(rows // CHUNK):
        r = slice(n * CHUNK, (n + 1) * CHUNK)
        for grp in range(GM_GROUPS):
            c = slice(grp * CHUNK, (grp + 1) * CHUNK)
            sv = jnp.dot(ws_ref[grp], v_ref[r, c], preferred_element_type=F32) + bs_ref[grp]
            y_ref[r, c] = (u[r, c] * sv).astype(BF16)
    o_ref[...] = x_ref[...] + jnp.dot(y_ref[...], wout_ref[...], preferred_element_type=F32)


def _gmlp(x, g, w_in, ln_g, ln_b, w_s, b_s, w_out, rows=512):
    t, d = x.shape
    row = pl.BlockSpec((rows, d), lambda i: (i, 0))
    causal = jnp.tril(jnp.ones((CHUNK, CHUNK), dtype=bool))
    ws = jnp.where(causal[None], w_s, jnp.zeros_like(w_s)).astype(BF16)
    bs = jnp.broadcast_to(b_s[:, :, None], (GM_GROUPS, CHUNK, CHUNK)).astype(F32)
    args = [x, g.reshape(1, d), w_in.astype(BF16), ln_g.reshape(1, GM_HALF), ln_b.reshape(1, GM_HALF),
            ws, bs, w_out.astype(BF16)]
    in_specs = [row] + [_resident(a.shape) for a in args[1:]]
    return pl.pallas_call(
        functools.partial(_gmlp_kernel, rows=rows),
        grid=(t // rows,),
        in_specs=in_specs,
        out_specs=row,
        out_shape=jax.ShapeDtypeStruct((t, d), F32),
        scratch_shapes=[pltpu.VMEM((rows, d), BF16), pltpu.VMEM((rows, GM_HALF), BF16),
                        pltpu.VMEM((rows, GM_HALF), BF16)],
        compiler_params=_params(1),
        name="gmlp",
    )(*args)


def _split_bf16(x, parts):
    out = []
    for _ in range(parts):
        t = x.astype(BF16).astype(F32)
        out.append(t)
        x = x - t
    return out


def _lane_select(lane, values):
    row = jnp.zeros(lane.shape, F32)
    for i, val in enumerate(values):
        row = jnp.where(lane == i, val, row)
    return row


def _half_norms_sq(x):
    ln = lax.broadcasted_iota(jnp.int32, x.shape, 1)
    x2 = x * x
    left = jnp.sum(jnp.where(ln < DA_HEAD_DIM, x2, 0.0), axis=-1, keepdims=True)
    right = jnp.sum(jnp.where(ln >= DA_HEAD_DIM, x2, 0.0), axis=-1, keepdims=True)
    return jnp.maximum(left, right)


def _attn_kernel(slopes_ref, q_ref, k_ref, v_ref, lq1_ref, lk1_ref, lq2_ref, lk2_ref, subg_ref, o_ref,
                 kx_ref, qx_ref, s0_ref, s1_ref, p0_ref, p1_ref, a0_ref, a1_ref, m_ref, l_ref, acc_ref, mask_ref,
                 knorm_ref, *, lambda_init, seq):
    hd = pl.program_id(1)
    qi = pl.program_id(2)
    slope = slopes_ref[hd] * LOG2_E
    lane = lax.broadcasted_iota(jnp.int32, (8, HEAD_W), 1)

    @pl.when(qi == 0)
    def _():
        rel = (lax.broadcasted_iota(jnp.int32, (ATTN_TQ, ATTN_TK), 0)
               - lax.broadcasted_iota(jnp.int32, (ATTN_TQ, ATTN_TK), 1))
        mask_ref[...] = jnp.where(rel >= 0, 0.0, MASK_VALUE)
        kx_ref[:, :HEAD_W] = k_ref[...]

        def key_norms(c, best):
            start = pl.multiple_of(c * NORM_ROWS, NORM_ROWS)
            k = k_ref[pl.ds(start, NORM_ROWS), :].astype(F32)
            n2 = jnp.dot((k * k).astype(BF16), halves, preferred_element_type=F32)
            return jnp.maximum(best, jnp.max(n2, axis=0, keepdims=True))

        dd = lax.broadcasted_iota(jnp.int32, (HEAD_W, HEAD_W), 0)
        cc = lax.broadcasted_iota(jnp.int32, (HEAD_W, HEAD_W), 1)
        halves = jnp.where((dd < DA_HEAD_DIM) == (cc % 2 == 0), 1.0, 0.0).astype(BF16)
        best = lax.fori_loop(0, seq // NORM_ROWS, key_norms, jnp.zeros((1, HEAD_W), F32))
        knorm_ref[...] = jnp.broadcast_to(jnp.max(best, axis=1, keepdims=True), knorm_ref.shape)

    @pl.when((pl.program_id(0) == 0) & (hd == 0) & (qi == 0))
    def _():
        def positions(c, carry):
            start = pl.multiple_of(c * ATTN_TK, ATTN_TK)
            pos = start + lax.broadcasted_iota(jnp.int32, (ATTN_TK, HEAD_W), 0)
            ln = lax.broadcasted_iota(jnp.int32, (ATTN_TK, HEAD_W), 1)
            hi = (pos >> 7).astype(F32)
            lo = (pos & 127).astype(F32)
            cols = jnp.where(ln >= 8, 0.0, jnp.where(ln >= 6, 1.0, jnp.where(ln % 2 == 0, hi, lo)))
            kx_ref[pl.ds(start, ATTN_TK), HEAD_W:] = cols.astype(BF16)
            return carry

        lax.fori_loop(0, seq // ATTN_TK, positions, 0)

    q = q_ref[...]
    ql = lax.broadcasted_iota(jnp.int32, q.shape, 1)
    qx_ref[:ATTN_TQ, :HEAD_W] = jnp.where(ql < DA_HEAD_DIM, q, jnp.zeros_like(q))
    qx_ref[ATTN_TQ:, :HEAD_W] = jnp.where(ql >= DA_HEAD_DIM, q, jnp.zeros_like(q))
    svec = jnp.full((8, HEAD_W), slope, F32)
    parts = _split_bf16(svec, 3)
    base = _split_bf16(-svec * (qi * ATTN_TQ).astype(F32), 2)
    coef = _lane_select(lane, [parts[0] * 128.0, parts[0], parts[1] * 128.0, parts[1],
                               parts[2] * 128.0, parts[2], base[0], base[1]])
    qx_ref[:, HEAD_W:] = jnp.broadcast_to(coef[:1], (2 * ATTN_TQ, HEAD_W)).astype(BF16)
    m_ref[...] = jnp.full(m_ref.shape, MASK_VALUE, F32)
    l_ref[...] = jnp.zeros(l_ref.shape, F32)
    acc_ref[...] = jnp.zeros(acc_ref.shape, F32)

    def scores(kj, s_ref):
        start = pl.multiple_of(kj * ATTN_TK, ATTN_TK)
        k = kx_ref[pl.ds(start, ATTN_TK), :]
        s_ref[...] = lax.dot_general(qx_ref[...], k, (((1,), (1,)), ((), ())), preferred_element_type=F32)

    def accumulate(kj, p_ref, alpha_ref):
        start = pl.multiple_of(kj * ATTN_TK, ATTN_TK)
        v = v_ref[pl.ds(start, ATTN_TK), :]
        acc_ref[...] = alpha_ref[...] * acc_ref[...] + jnp.dot(p_ref[...], v, preferred_element_type=F32)

    def softmax(s_ref, p_ref, alpha_ref, masked):
        def block_scores():
            if masked:
                return s_ref[...] + jnp.concatenate([mask_ref[...], mask_ref[...]], axis=0)
            return s_ref[...]
        m_old = m_ref[...]
        m_new = jnp.maximum(m_old, jnp.max(block_scores(), axis=-1, keepdims=True))
        alpha = jnp.exp2(m_old - m_new)
        alpha_ref[...] = alpha
        m_ref[...] = m_new
        s = block_scores()
        ps = [jnp.exp2(s[:, j * HEAD_W:(j + 1) * HEAD_W] - m_new) for j in range(ATTN_TK // HEAD_W)]
        l_ref[...] = alpha * l_ref[...] + functools.reduce(lambda a, b: a + b, ps)
        p_ref[...] = jnp.concatenate(ps, axis=1).astype(BF16)

    qnorm = jnp.max(_half_norms_sq(q.astype(F32)), axis=0, keepdims=True)
    bound = jnp.sqrt(qnorm * knorm_ref[:1, :1]) * 1.02 + 1.0

    s_bufs, p_bufs, a_bufs = (s0_ref, s1_ref), (p0_ref, p1_ref), (a0_ref, a1_ref)

    def step(t, par, last=False):
        accumulate(qi - (t - 1), p_bufs[1 - par], a_bufs[1 - par])
        softmax(s_bufs[par], p_bufs[par], a_bufs[par], masked=False)
        if not last:
            scores(jnp.maximum(qi - (t + 1), 0), s_bufs[1 - par])

    scores(qi, s0_ref)
    softmax(s0_ref, p0_ref, a0_ref, masked=True)
    scores(jnp.maximum(qi - 1, 0), s1_ref)

    m_min = jnp.min(jnp.min(m_ref[...], axis=0, keepdims=True), axis=1, keepdims=True)
    reach = (bound - m_min + SKIP_LOG2) / slope
    first = jnp.floor(((qi * ATTN_TQ).astype(F32) - reach) * (1.0 / ATTN_TK))
    n_far = qi - jnp.clip(first, 0.0, qi.astype(F32)).astype(jnp.int32)[0, 0]

    def steps(first, count):
        for i in range(count):
            step(first + i, (i + 1) % 2)

    def octet(u, carry):
        steps(8 * u + 1, 8)
        return carry

    def quad(u, carry):
        steps(done8 + 4 * u + 1, 4)
        return carry

    def pair(u, carry):
        steps(done + 2 * u + 1, 2)
        return carry

    done8 = (n_far // 8) * 8
    done = (n_far // 4) * 4
    lax.fori_loop(0, n_far // 8, octet, 0)
    lax.fori_loop(0, (n_far - done8) // 4, quad, 0)
    lax.fori_loop(0, (n_far - done) // 2, pair, 0)

    @pl.when(n_far % 2 == 1)
    def _():
        step(n_far, 1, last=True)
        accumulate(qi - n_far, p1_ref, a1_ref)

    @pl.when(n_far % 2 == 0)
    def _():
        accumulate(qi - n_far, p0_ref, a0_ref)

    lam = (jnp.exp(jnp.sum(lq1_ref[...] * lk1_ref[...], axis=-1, keepdims=True))
           - jnp.exp(jnp.sum(lq2_ref[...] * lk2_ref[...], axis=-1, keepdims=True))
           + lambda_init)
    l = jnp.sum(l_ref[...], axis=-1, keepdims=True)
    o = acc_ref[...] / l
    o = o[:ATTN_TQ] - lam * o[ATTN_TQ:]
    o_ref[...] = (_rms(o, subg_ref[...]) * (1.0 - lambda_init)).astype(BF16)


def _attention(q, kv, slopes, lam_q1, lam_k1, lam_q2, lam_k2, subln_g, batch, seq, lambda_init):
    nq = seq // ATTN_TQ
    vec = lambda a: a.reshape(1, -1).astype(F32)
    lam_spec = _resident((1, DA_HEAD_DIM))
    return pl.pallas_call(
        functools.partial(_attn_kernel, lambda_init=lambda_init, seq=seq),
        grid=(batch, DA_HEADS, nq),
        in_specs=[
            pl.BlockSpec(memory_space=pltpu.SMEM),
            pl.BlockSpec((None, ATTN_TQ, HEAD_W), lambda b, h, i: (b, i, h)),
            pl.BlockSpec((None, seq, HEAD_W), lambda b, h, i: (b, 0, h)),
            pl.BlockSpec((None, seq, HEAD_W), lambda b, h, i: (b, 0, DA_HEADS + h)),
            lam_spec, lam_spec, lam_spec, lam_spec,
            _resident((1, HEAD_W)),
        ],
        out_specs=pl.BlockSpec((None, ATTN_TQ, HEAD_W), lambda b, h, i: (b, i, h)),
        out_shape=jax.ShapeDtypeStruct((batch, seq, DA_HEADS * HEAD_W), BF16),
        scratch_shapes=[
            pltpu.VMEM((seq, 2 * HEAD_W), BF16),
            pltpu.VMEM((2 * ATTN_TQ, 2 * HEAD_W), BF16),
            pltpu.VMEM((2 * ATTN_TQ, ATTN_TK), F32),
            pltpu.VMEM((2 * ATTN_TQ, ATTN_TK), F32),
            pltpu.VMEM((2 * ATTN_TQ, ATTN_TK), BF16),
            pltpu.VMEM((2 * ATTN_TQ, ATTN_TK), BF16),
            pltpu.VMEM((2 * ATTN_TQ, HEAD_W), F32),
            pltpu.VMEM((2 * ATTN_TQ, HEAD_W), F32),
            pltpu.VMEM((2 * ATTN_TQ, HEAD_W), F32),
            pltpu.VMEM((2 * ATTN_TQ, HEAD_W), F32),
            pltpu.VMEM((2 * ATTN_TQ, HEAD_W), F32),
            pltpu.VMEM((ATTN_TQ, ATTN_TK), F32),
            pltpu.VMEM((8, HEAD_W), F32),
        ],
        compiler_params=_params(3),
        name="diff_attn",
    )(slopes, q, kv, kv, vec(lam_q1), vec(lam_k1), vec(lam_q2), vec(lam_k2), vec(subln_g))


def kernel(x, ffn_norm1_g, ffn1_w_gate, ffn1_w_up, ffn1_w_down, mix_norm_g, ffn_norm2_g, ffn2_w_gate, ffn2_w_up, ffn2_w_down, gm_w_in, gm_ln_g, gm_ln_b, gm_w_s, gm_b_s, gm_w_out, kv_norm_g, w_k, w_v, da_w_q, da_lam_q1, da_lam_k1, da_lam_q2, da_lam_k2, da_subln_g, da_w_o, final_norm_g):
    batch, seq, d = x.shape
    depth = ffn_norm1_g.shape[0]
    assert (d, depth, N_A_LAYERS) == (D_MODEL, 2, 1)
    xs = x.reshape(batch * seq, d)

    def ffn1(l, x_in, **fused):
        return _ffn(x_in, ffn_norm1_g[l], ffn1_w_gate[l], ffn1_w_up[l], ffn1_w_down[l], **fused)

    def ffn2(l, x_in, **fused):
        return _ffn(x_in, ffn_norm2_g[l], ffn2_w_gate[l], ffn2_w_up[l], ffn2_w_down[l], **fused)

    xs = ffn1(0, xs)
    xs = _gmlp(xs, mix_norm_g[0], gm_w_in[0], gm_ln_g[0], gm_ln_b[0], gm_w_s[0], gm_b_s[0], gm_w_out[0])
    xs, kv = ffn2(0, xs, proj=(kv_norm_g, jnp.concatenate([w_k, w_v], axis=1), 1.0))

    lambda_init = 0.8 - 0.6 * math.exp(-0.3 * 1)
    xs, q = ffn1(1, xs, proj=(mix_norm_g[1], da_w_q[0], (DA_HEAD_DIM ** -0.5) * LOG2_E))
    slopes = jnp.exp2(-8.0 * (jnp.arange(DA_HEADS, dtype=F32) + 1.0) / DA_HEADS)
    a = _attention(q.reshape(batch, seq, -1), kv.reshape(batch, seq, -1), slopes,
                   da_lam_q1[0], da_lam_k1[0], da_lam_q2[0], da_lam_k2[0], da_subln_g[0],
                   batch, seq, lambda_init)
    xs = ffn2(1, xs, pre=(a.reshape(batch * seq, -1), da_w_o[0]), final_g=final_norm_g)
    return xs.reshape(batch, seq, d)
```
